```python
import math
import jax, jax.numpy as jnp
from jax import lax
import numpy as np

D_MODEL = 2048
BATCH = 2
SEQ = 4096
DEPTH = 4
DEC_BATCH = 8
DEC_SEQ = 4
PAST_LEN = 16384
PAGE_SIZE = 128

N_MIXERS = 2
SB_HEADS = 16
SB_HEAD_DIM = D_MODEL // SB_HEADS
DIFF_HEAD_DIM = 128
DIFF_HEADS = D_MODEL // (2 * DIFF_HEAD_DIM)
D_FF = -(-8 * D_MODEL // (3 * 256)) * 256
ROPE_THETA = 10000.0
Q_BLOCK = 128
LN_EPS = 1e-5
SUBLN_EPS = 1e-5
DEEPNORM_ALPHA = (2 * DEPTH) ** 0.25
DEEPNORM_BETA = (8 * DEPTH) ** -0.25
N_SB_LAYERS = (DEPTH + 1) // 2
N_DIFF_LAYERS = DEPTH // 2

kernel_name = 'stickbreak_diffattn_hybrid_step'


def layer_norm(x, g, b):
    xf = x.astype(jnp.float32)
    mu = xf.mean(-1, keepdims=True)
    var = jnp.square(xf - mu).mean(-1, keepdims=True)
    return ((xf - mu) * lax.rsqrt(var + LN_EPS) * g.astype(jnp.float32) + b.astype(jnp.float32)).astype(x.dtype)


def post_norm(h, sub, g, b):
    return layer_norm(DEEPNORM_ALPHA * h + sub, g, b)


def swiglu(h, w_gate_up, w_down):
    gate, up = jnp.split(h @ w_gate_up, 2, axis=-1)
    return (jax.nn.silu(gate) * up) @ w_down


def rope(x, pos):
    half = x.shape[-1] // 2
    inv = ROPE_THETA ** (-jnp.arange(half, dtype=jnp.float32) / half)
    ang = pos.astype(jnp.float32)[:, None] * inv[None, :]
    cos = jnp.cos(ang)[None, :, None, None, :]
    sin = jnp.sin(ang)[None, :, None, None, :]
    xf = x.astype(jnp.float32)
    x1, x2 = xf[..., :half], xf[..., half:]
    return jnp.concatenate([x1 * cos - x2 * sin, x2 * cos + x1 * sin], axis=-1).astype(x.dtype)


def stick_breaking_weights(z, valid):
    log_1m = jnp.where(valid, jax.nn.log_sigmoid(-z), 0.0)
    tail = lax.cumsum(log_1m, axis=z.ndim - 1, reverse=True) - log_1m
    return jnp.where(valid, jnp.exp(jax.nn.log_sigmoid(z) + tail), 0.0)


def sb_attend_prompt(q, k, v):
    B, S, H, Dh = q.shape
    nblk = S // Q_BLOCK
    scale = SB_HEAD_DIM ** -0.5
    qb = q.reshape(B, nblk, Q_BLOCK, H, Dh).transpose(1, 0, 2, 3, 4)
    kpos = jnp.arange(S)

    def block(args):
        qi, i = args
        z = jnp.einsum('bqhd,bkhd->bhqk', qi, k, preferred_element_type=jnp.float32) * scale
        qpos = i * Q_BLOCK + jnp.arange(Q_BLOCK)
        valid = kpos[None, :] < qpos[:, None]
        a = stick_breaking_weights(z, valid)
        return jnp.einsum('bhqk,bkhd->bqhd', a.astype(v.dtype), v)

    out = lax.map(block, (qb, jnp.arange(nblk)))
    return out.transpose(1, 0, 2, 3, 4).reshape(B, S, H, Dh)


def sb_attend_sample(q, k, v, k_past, v_past):
    T = q.shape[1]
    P = k_past.shape[1]
    scale = SB_HEAD_DIM ** -0.5
    z_past = jnp.einsum('bqhd,bkhd->bhqk', q, k_past, preferred_element_type=jnp.float32)
    z_new = jnp.einsum('bqhd,bkhd->bhqk', q, k, preferred_element_type=jnp.float32)
    z = jnp.concatenate([z_past, z_new], axis=-1) * scale
    valid = jnp.concatenate([jnp.ones((T, P), bool), jnp.tril(jnp.ones((T, T), bool), -1)], axis=-1)
    a = stick_breaking_weights(z, valid).astype(v.dtype)
    return (jnp.einsum('bhqk,bkhd->bqhd', a[..., :P], v_past)
            + jnp.einsum('bhqk,bkhd->bqhd', a[..., P:], v))


def diff_weights(z, valid, lam):
    p = jax.nn.softmax(jnp.where(valid, z, -jnp.inf), axis=-1)
    return p[..., 0, :, :] - lam * p[..., 1, :, :]


def diff_attend_prompt(q, k, v, lam):
    B, S, H, _, d = q.shape
    nblk = S // Q_BLOCK
    scale = DIFF_HEAD_DIM ** -0.5
    qb = q.reshape(B, nblk, Q_BLOCK, H, 2, d).transpose(1, 0, 2, 3, 4, 5)
    kpos = jnp.arange(S)

    def block(args):
        qi, i = args
        z = jnp.einsum('bqhmd,bkhmd->bhmqk', qi, k, preferred_element_type=jnp.float32) * scale
        qpos = i * Q_BLOCK + jnp.arange(Q_BLOCK)
        valid = kpos[None, :] <= qpos[:, None]
        w = diff_weights(z, valid, lam)
        return jnp.einsum('bhqk,bkhe->bqhe', w.astype(v.dtype), v)

    out = lax.map(block, (qb, jnp.arange(nblk)))
    return out.transpose(1, 0, 2, 3, 4).reshape(B, S, H, 2 * d)


def diff_attend_sample(q, k, v, k_past, v_past, lam):
    T = q.shape[1]
    P = k_past.shape[1]
    scale = DIFF_HEAD_DIM ** -0.5
    z_past = jnp.einsum('bqhmd,bkhmd->bhmqk', q, k_past, preferred_element_type=jnp.float32)
    z_new = jnp.einsum('bqhmd,bkhmd->bhmqk', q, k, preferred_element_type=jnp.float32)
    z = jnp.concatenate([z_past, z_new], axis=-1) * scale
    valid = jnp.concatenate([jnp.ones((T, P), bool), jnp.tril(jnp.ones((T, T), bool))], axis=-1)
    w = diff_weights(z, valid, lam).astype(v.dtype)
    return (jnp.einsum('bhqk,bkhe->bqhe', w[..., :P], v_past)
            + jnp.einsum('bhqk,bkhe->bqhe', w[..., P:], v))


def diff_head_norm(o, g, lam_init):
    of = o.astype(jnp.float32)
    of = of * lax.rsqrt(jnp.mean(jnp.square(of), axis=-1, keepdims=True) + SUBLN_EPS)
    return (of * g.astype(jnp.float32) * (1.0 - lam_init)).astype(o.dtype)


def setup_inputs(seed: int = 0) -> dict:
    key = jax.random.key(seed)
    ks = jax.random.split(key, 20)
    n_pages = PAST_LEN // PAGE_SIZE
    n_pool = (5 * DEC_BATCH * n_pages) // 4
    f32 = jnp.float32
    d_in = D_MODEL ** -0.5
    x_prompt = jax.random.normal(ks[0], (BATCH, SEQ, D_MODEL), f32)
    x_sample = jax.random.normal(ks[1], (DEC_BATCH, DEC_SEQ, D_MODEL), f32)
    cache_sb_k = jax.random.normal(ks[2], (N_SB_LAYERS, n_pool, PAGE_SIZE, SB_HEADS, SB_HEAD_DIM), f32)
    cache_sb_v = DEEPNORM_BETA * jax.random.normal(ks[3], (N_SB_LAYERS, n_pool, PAGE_SIZE, SB_HEADS, SB_HEAD_DIM), f32)
    cache_diff_k = jax.random.normal(ks[4], (N_DIFF_LAYERS, n_pool, PAGE_SIZE, DIFF_HEADS, 2, DIFF_HEAD_DIM), f32)
    cache_diff_v = DEEPNORM_BETA * jax.random.normal(ks[5], (N_DIFF_LAYERS, n_pool, PAGE_SIZE, DIFF_HEADS, 2 * DIFF_HEAD_DIM), f32)
    page_table = jax.random.permutation(ks[6], n_pool)[:DEC_BATCH * n_pages].reshape(DEC_BATCH, n_pages).astype(jnp.int32)
    w_qk = d_in * jax.random.normal(ks[7], (DEPTH, D_MODEL, 2 * D_MODEL), f32)
    w_v = DEEPNORM_BETA * d_in * jax.random.normal(ks[8], (DEPTH, D_MODEL, D_MODEL), f32)
    w_qkv = jnp.concatenate([w_qk, w_v], axis=-1)
    w_o = DEEPNORM_BETA * d_in * jax.random.normal(ks[9], (DEPTH, D_MODEL, D_MODEL), f32)
    diff_lambda = 0.1 * jax.random.normal(ks[10], (N_DIFF_LAYERS, 4, DIFF_HEAD_DIM), f32)
    diff_subln_g = 1.0 + 0.02 * jax.random.normal(ks[11], (N_DIFF_LAYERS, 2 * DIFF_HEAD_DIM), f32)
    ln_mix_g = 1.0 + 0.02 * jax.random.normal(ks[12], (DEPTH, D_MODEL), f32)
    ln_mix_b = 0.02 * jax.random.normal(ks[13], (DEPTH, D_MODEL), f32)
    w_gate_up = DEEPNORM_BETA * d_in * jax.random.normal(ks[14], (DEPTH, D_MODEL, 2 * D_FF), f32)
    w_down = DEEPNORM_BETA * (D_FF ** -0.5) * jax.random.normal(ks[15], (DEPTH, D_FF, D_MODEL), f32)
    ln_ffn_g = 1.0 + 0.02 * jax.random.normal(ks[16], (DEPTH, D_MODEL), f32)
    ln_ffn_b = 0.02 * jax.random.normal(ks[17], (DEPTH, D_MODEL), f32)
    return {'x_prompt': x_prompt, 'x_sample': x_sample,
            'cache_sb_k': cache_sb_k, 'cache_sb_v': cache_sb_v,
            'cache_diff_k': cache_diff_k, 'cache_diff_v': cache_diff_v,
            'page_table': page_table,
            'w_qkv': w_qkv, 'w_o': w_o, 'diff_lambda': diff_lambda, 'diff_subln_g': diff_subln_g,
            'ln_mix_g': ln_mix_g, 'ln_mix_b': ln_mix_b, 'w_gate_up': w_gate_up, 'w_down': w_down,
            'ln_ffn_g': ln_ffn_g, 'ln_ffn_b': ln_ffn_b}


def reference(x_prompt, x_sample, cache_sb_k, cache_sb_v, cache_diff_k, cache_diff_v, page_table,
              w_qkv, w_o, diff_lambda, diff_subln_g, ln_mix_g, ln_mix_b, w_gate_up, w_down,
              ln_ffn_g, ln_ffn_b):
    B, S, D = x_prompt.shape
    Bd, T, _ = x_sample.shape
    P = page_table.shape[1] * PAGE_SIZE
    pos_p = jnp.arange(S)
    pos_s = P + jnp.arange(T)
    hp, hs = x_prompt, x_sample
    sb_kp, sb_vp, sb_ks, sb_vs = [], [], [], []
    df_kp, df_vp, df_ks, df_vs = [], [], [], []
    for i in range(DEPTH):
        j = i // N_MIXERS
        qp, kp, vp = jnp.split(hp @ w_qkv[i], 3, axis=-1)
        qs, kss, vs = jnp.split(hs @ w_qkv[i], 3, axis=-1)
        if i % N_MIXERS == 0:
            hd = (SB_HEADS, SB_HEAD_DIM)
            qp, kp, vp = qp.reshape(B, S, *hd), kp.reshape(B, S, *hd), vp.reshape(B, S, *hd)
            qs, kss, vs = qs.reshape(Bd, T, *hd), kss.reshape(Bd, T, *hd), vs.reshape(Bd, T, *hd)
            k_past = cache_sb_k[j, page_table].reshape(Bd, P, *hd)
            v_past = cache_sb_v[j, page_table].reshape(Bd, P, *hd)
            op = sb_attend_prompt(qp, kp, vp)
            os_ = sb_attend_sample(qs, kss, vs, k_past, v_past)
            sb_kp.append(kp); sb_vp.append(vp); sb_ks.append(kss); sb_vs.append(vs)
        else:
            qk_shape = (DIFF_HEADS, 2, DIFF_HEAD_DIM)
            v_shape = (DIFF_HEADS, 2 * DIFF_HEAD_DIM)
            qp = rope(qp.reshape(B, S, *qk_shape), pos_p)
            kp = rope(kp.reshape(B, S, *qk_shape), pos_p)
            vp = vp.reshape(B, S, *v_shape)
            qs = rope(qs.reshape(Bd, T, *qk_shape), pos_s)
            kss = rope(kss.reshape(Bd, T, *qk_shape), pos_s)
            vs = vs.reshape(Bd, T, *v_shape)
            k_past = cache_diff_k[j, page_table].reshape(Bd, P, *qk_shape)
            v_past = cache_diff_v[j, page_table].reshape(Bd, P, *v_shape)
            lam_init = 0.8 - 0.6 * math.exp(-0.3 * i)
            lq1, lk1, lq2, lk2 = diff_lambda[j].astype(jnp.float32)
            lam = jnp.exp(jnp.sum(lq1 * lk1)) - jnp.exp(jnp.sum(lq2 * lk2)) + lam_init
            op = diff_head_norm(diff_attend_prompt(qp, kp, vp, lam), diff_subln_g[j], lam_init)
            os_ = diff_head_norm(diff_attend_sample(qs, kss, vs, k_past, v_past, lam), diff_subln_g[j], lam_init)
            df_kp.append(kp); df_vp.append(vp); df_ks.append(kss); df_vs.append(vs)
        hp = post_norm(hp, op.reshape(B, S, D) @ w_o[i], ln_mix_g[i], ln_mix_b[i])
        hs = post_norm(hs, os_.reshape(Bd, T, D) @ w_o[i], ln_mix_g[i], ln_mix_b[i])
        hp = post_norm(hp, swiglu(hp, w_gate_up[i], w_down[i]), ln_ffn_g[i], ln_ffn_b[i])
        hs = post_norm(hs, swiglu(hs, w_gate_up[i], w_down[i]), ln_ffn_g[i], ln_ffn_b[i])
    sb_k_prompt = jnp.stack(sb_kp)
    sb_v_prompt = jnp.stack(sb_vp)
    diff_k_prompt = jnp.stack(df_kp)
    diff_v_prompt = jnp.stack(df_vp)
    sb_k_sample = jnp.stack(sb_ks)
    sb_v_sample = jnp.stack(sb_vs)
    diff_k_sample = jnp.stack(df_ks)
    diff_v_sample = jnp.stack(df_vs)
    return (hp, hs, sb_k_prompt, sb_v_prompt, diff_k_prompt, diff_v_prompt,
            sb_k_sample, sb_v_sample, diff_k_sample, diff_v_sample)
```

```python
import functools
import math

import jax
import jax.numpy as jnp
from jax import lax
from jax.experimental import pallas as pl
from jax.experimental.pallas import tpu as pltpu

F32 = jnp.float32
BF16 = jnp.bfloat16

ROPE_THETA = 10000.0
LN_EPS = 1e-5
SUBLN_EPS = 1e-5
SB_HEAD_DIM = 128
DIFF_HEAD_DIM = 128
N_MIXERS = 2

LANES = 128
VMEM_LIMIT_BYTES = 56 * 1024 * 1024

NEG_BIG = -1e30


def _cparams(sem):
    return pltpu.CompilerParams(dimension_semantics=sem, vmem_limit_bytes=VMEM_LIMIT_BYTES)


def _dot(a, b):
    return jnp.dot(a, b, preferred_element_type=F32)


def _dot_nt(a, b):
    return lax.dot_general(a, b, (((1,), (1,)), ((), ())), preferred_element_type=F32)


def _rope_cols(a, cos, sin):
    return a * cos + pltpu.roll(a, LANES // 2, 1) * sin


def _qkv_kernel(*refs, rope, tn):
    if rope:
        x_ref, wq_ref, wk_ref, wv_ref, cos_ref, sin_ref, q_ref, kf_ref, kb_ref, vf_ref, vb_ref = refs
    else:
        x_ref, wq_ref, wk_ref, wv_ref, q_ref, kf_ref, kb_ref, vf_ref, vb_ref = refs
    x = x_ref[...]
    q = _dot(x, wq_ref[...])
    k = _dot(x, wk_ref[...])
    v = _dot(x, wv_ref[...])
    vf_ref[...] = v
    vb_ref[...] = v.astype(BF16)
    if rope:
        cos = cos_ref[...]
        sin = sin_ref[...]
        for g in range(tn // LANES):
            sl = slice(g * LANES, (g + 1) * LANES)
            qg = _rope_cols(q[:, sl], cos, sin)
            kg = _rope_cols(k[:, sl], cos, sin)
            q_ref[:, sl] = qg.astype(BF16)
            kf_ref[:, sl] = kg
            kb_ref[:, sl] = kg.astype(BF16)
    else:
        q_ref[...] = q.astype(BF16)
        kf_ref[...] = k
        kb_ref[...] = k.astype(BF16)


def _qkv_proj(x, w, rope_tabs, *, tm, tn, n_pos_blocks):
    M, D = x.shape
    nj = D // tn
    rope = rope_tabs is not None
    in_specs = [
        pl.BlockSpec((tm, D), lambda i, j: (i, 0)),
        pl.BlockSpec((D, tn), lambda i, j: (0, j)),
        pl.BlockSpec((D, tn), lambda i, j: (0, j + nj)),
        pl.BlockSpec((D, tn), lambda i, j: (0, j + 2 * nj)),
    ]
    args = [x, w, w, w]
    if rope:
        tab_spec = pl.BlockSpec((tm, LANES), lambda i, j: (i % n_pos_blocks, 0))
        in_specs += [tab_spec, tab_spec]
        args += list(rope_tabs)
    out_spec = pl.BlockSpec((tm, tn), lambda i, j: (i, j))
    return pl.pallas_call(
        functools.partial(_qkv_kernel, rope=rope, tn=tn),
        grid=(M // tm, nj),
        in_specs=in_specs,
        out_specs=[out_spec] * 5,
        out_shape=[jax.ShapeDtypeStruct((M, D), dt) for dt in (BF16, F32, BF16, F32, BF16)],
        compiler_params=_cparams(("parallel", "arbitrary")),
        name="qkv_rope" if rope else "qkv",
    )(*args)


def _suffix_matrix(n):
    row = lax.broadcasted_iota(jnp.int32, (n, n), 0)
    col = lax.broadcasted_iota(jnp.int32, (n, n), 1)
    return jnp.where(row > col, 1.0, 0.0).astype(BF16)


def _suffix_sum(x, u):
    hi = x.astype(BF16)
    lo = (x - hi.astype(F32)).astype(BF16)
    return _dot(hi, u) + _dot(lo, u)


def _sb_tile(z, valid, u, carry):
    l1p = jnp.log(1.0 + jnp.exp(-jnp.abs(z)))
    log_sig = jnp.minimum(z, 0.0) - l1p
    log_1m = log_sig - z
    if valid is not None:
        log_1m = jnp.where(valid, log_1m, 0.0)
    tail = _suffix_sum(log_1m, u)
    a = jnp.exp(log_sig + tail + carry)
    if valid is not None:
        a = jnp.where(valid, a, 0.0)
    return a, carry + jnp.sum(log_1m, axis=1, keepdims=True)


def _sb_prompt_kernel(q_ref, k_ref, v_ref, o_ref, *, blk, scale):
    i = pl.program_id(2)
    q = q_ref[...]
    u = _suffix_matrix(blk)
    row = lax.broadcasted_iota(jnp.int32, (blk, blk), 0)
    col = lax.broadcasted_iota(jnp.int32, (blk, blk), 1)

    def tile(j, carry, acc, valid):
        start = pl.multiple_of(j * blk, blk)
        k = k_ref[pl.ds(start, blk), :]
        v = v_ref[pl.ds(start, blk), :]
        z = _dot_nt(q, k) * scale
        a, carry = _sb_tile(z, valid, u, carry)
        return carry, acc + _dot(a.astype(BF16), v)

    carry = jnp.zeros((blk, 1), F32)
    acc = jnp.zeros((blk, q.shape[1]), F32)
    carry, acc = tile(i, carry, acc, col < row)

    def body(jj, st):
        return tile(i - 1 - jj, st[0], st[1], None)

    carry, acc = lax.fori_loop(0, i, body, (carry, acc))
    o_ref[...] = acc.astype(o_ref.dtype)


def _sb_attend_prompt(q, k, v, *, B, S, blk):
    M, D = q.shape
    H = D // SB_HEAD_DIM
    nq = S // blk
    return pl.pallas_call(
        functools.partial(_sb_prompt_kernel, blk=blk, scale=SB_HEAD_DIM ** -0.5),
        grid=(B, H, nq),
        in_specs=[
            pl.BlockSpec((blk, SB_HEAD_DIM), lambda b, h, i: (b * nq + i, h)),
            pl.BlockSpec((S, SB_HEAD_DIM), lambda b, h, i: (b, h)),
            pl.BlockSpec((S, SB_HEAD_DIM), lambda b, h, i: (b, h)),
        ],
        out_specs=pl.BlockSpec((blk, SB_HEAD_DIM), lambda b, h, i: (b * nq + i, h)),
        out_shape=jax.ShapeDtypeStruct((M, D), BF16),
        compiler_params=_cparams(("parallel", "parallel", "arbitrary")),
        name="sb_prompt",
    )(q, k, v)


def _diff_lambda(lam_ref, lam_init):
    lv = lam_ref[...]
    s1 = jnp.sum(lv[0:1] * lv[1:2], axis=1, keepdims=True)
    s2 = jnp.sum(lv[2:3] * lv[3:4], axis=1, keepdims=True)
    return jnp.exp(s1) - jnp.exp(s2) + lam_init


def _softmax_step(z, m, l):
    m_new = jnp.maximum(m, jnp.max(z, axis=1, keepdims=True))
    alpha = jnp.exp(m - m_new)
    p = jnp.exp(z - m_new)
    return p, alpha, m_new, alpha * l + jnp.sum(p, axis=1, keepdims=True)


def _head_norm(o, g, lam_init):
    ms = jnp.mean(o * o, axis=1, keepdims=True)
    return o * lax.rsqrt(ms + SUBLN_EPS) * g * (1.0 - lam_init)


def _diff_prompt_kernel(q_ref, k_ref, v_ref, lam_ref, g_ref, o_ref, *, blk, scale, lam_init):
    i = pl.program_id(2)
    d = DIFF_HEAD_DIM
    q = q_ref[...]
    q1, q2 = q[:, :d], q[:, d:]
    row = lax.broadcasted_iota(jnp.int32, (blk, blk), 0)
    col = lax.broadcasted_iota(jnp.int32, (blk, blk), 1)

    def tile(j, st, valid):
        m1, l1, a1, m2, l2, a2 = st
        start = pl.multiple_of(j * blk, blk)
        k = k_ref[pl.ds(start, blk), :]
        v = v_ref[pl.ds(start, blk), :]
        z1 = _dot_nt(q1, k[:, :d]) * scale
        z2 = _dot_nt(q2, k[:, d:]) * scale
        if valid is not None:
            z1 = jnp.where(valid, z1, NEG_BIG)
            z2 = jnp.where(valid, z2, NEG_BIG)
        p1, al1, m1, l1 = _softmax_step(z1, m1, l1)
        p2, al2, m2, l2 = _softmax_step(z2, m2, l2)
        a1 = al1 * a1 + _dot(p1.astype(BF16), v)
        a2 = al2 * a2 + _dot(p2.astype(BF16), v)
        return m1, l1, a1, m2, l2, a2

    mi = jnp.full((blk, 1), NEG_BIG, F32)
    li = jnp.zeros((blk, 1), F32)
    ai = jnp.zeros((blk, 2 * d), F32)
    st = tile(i, (mi, li, ai, mi, li, ai), col <= row)

    def body(jj, st):
        return tile(i - 1 - jj, st, None)

    m1, l1, a1, m2, l2, a2 = lax.fori_loop(0, i, body, st)
    lam = _diff_lambda(lam_ref, lam_init)
    o = a1 / l1 - lam * (a2 / l2)
    o_ref[...] = _head_norm(o, g_ref[...], lam_init).astype(o_ref.dtype)


def _diff_attend_prompt(q, k, v, lam_par, g, *, B, S, blk, lam_init):
    M, D = q.shape
    hw = 2 * DIFF_HEAD_DIM
    H = D // hw
    nq = S // blk
    return pl.pallas_call(
        functools.partial(_diff_prompt_kernel, blk=blk, scale=DIFF_HEAD_DIM ** -0.5, lam_init=lam_init),
        grid=(B, H, nq),
        in_specs=[
            pl.BlockSpec((blk, hw), lambda b, h, i: (b * nq + i, h)),
            pl.BlockSpec((S, hw), lambda b, h, i: (b, h)),
            pl.BlockSpec((S, hw), lambda b, h, i: (b, h)),
            pl.BlockSpec(lam_par.shape, lambda b, h, i: (0, 0)),
            pl.BlockSpec((1, hw), lambda b, h, i: (0, 0)),
        ],
        out_specs=pl.BlockSpec((blk, hw), lambda b, h, i: (b * nq + i, h)),
        out_shape=jax.ShapeDtypeStruct((M, D), BF16),
        compiler_params=_cparams(("parallel", "parallel", "arbitrary")),
        name="diff_prompt",
    )(q, k, v, lam_par, g)


def _head_views(cache_ref, layer, pid):
    if len(cache_ref.shape) == 5:
        return [cache_ref.at[layer, pid, :, h, :] for h in range(cache_ref.shape[3])]
    return [cache_ref.at[layer, pid, :, h, m, :]
            for h in range(cache_ref.shape[3]) for m in range(cache_ref.shape[4])]


def _decode_kernel(*refs, diff, T, pages_per_step, page, layer, n_pages, scale, lam_init):
    G = pages_per_step
    if diff:
        (pt_ref, q_ref, kn_ref, vn_ref, lam_ref, g_ref), rest = refs[:6], refs[6:]
    else:
        (pt_ref, q_ref, kn_ref, vn_ref), rest = refs[:4], refs[4:]
    ck_ref, cv_ref, o_ref, qbd_ref, acc_ref, s0_ref, s1_ref, kpad_ref, vpad_ref, kbuf, vbuf, sem = rest
    step = pl.program_id(1)
    n_steps = n_pages // G
    lin = pl.program_id(0) * n_steps + step
    slot = lax.rem(lin, 2)
    R, D = qbd_ref.shape
    slot_w = D // (R // T)
    out_w = 2 * slot_w if diff else slot_w
    u = None if diff else _suffix_matrix(page)

    def page_copies(lin_step, sl):
        bb = lax.div(lin_step, n_steps)
        ss = lax.rem(lin_step, n_steps)
        cps = []
        for r in range(G):
            pid = pt_ref[bb, n_pages - 1 - (ss * G + r)]
            for c_ref, buf, si in ((ck_ref, kbuf, 0), (cv_ref, vbuf, 1)):
                views = _head_views(c_ref, layer, pid)
                w = D // len(views)
                for c, src in enumerate(views):
                    cps.append(pltpu.make_async_copy(src, buf.at[sl, r, :, pl.ds(c * w, w)], sem.at[si, sl]))
        return cps

    @pl.when(lin == 0)
    def _():
        for cp in page_copies(lin, slot):
            cp.start()

    @pl.when(lin + 1 < pl.num_programs(0) * n_steps)
    def _():
        for cp in page_copies(lin + 1, 1 - slot):
            cp.start()

    def attend(kf, vf, valid):
        z = _dot_nt(qbd_ref[...], kf.astype(BF16)) * scale
        vb = vf.astype(BF16)
        if diff:
            if valid is not None:
                z = jnp.where(valid, z, NEG_BIG)
            p, alpha, m_new, l_new = _softmax_step(z, s0_ref[...], s1_ref[...])
            s0_ref[...] = m_new
            s1_ref[...] = l_new
            acc_ref[...] = alpha * acc_ref[...] + _dot(p.astype(BF16), vb)
        else:
            a, carry = _sb_tile(z, valid, u, s0_ref[...])
            s0_ref[...] = carry
            acc_ref[...] += _dot(a.astype(BF16), vb)

    @pl.when(step == 0)
    def _():
        rg = lax.broadcasted_iota(jnp.int32, (R, D), 0) // T
        cg = lax.broadcasted_iota(jnp.int32, (R, D), 1) // slot_w
        qbd_ref[...] = jnp.where(rg == cg, q_ref[...], jnp.zeros((), BF16))
        acc_ref[...] = jnp.zeros_like(acc_ref)
        s0_ref[...] = jnp.full(s0_ref.shape, NEG_BIG if diff else 0.0, F32)
        s1_ref[...] = jnp.zeros_like(s1_ref)
        kpad_ref[...] = jnp.zeros_like(kpad_ref)
        vpad_ref[...] = jnp.zeros_like(vpad_ref)
        kpad_ref[0:kn_ref.shape[0], :] = kn_ref[...]
        vpad_ref[0:vn_ref.shape[0], :] = vn_ref[...]
        t = lax.broadcasted_iota(jnp.int32, (R, page), 0) % T
        s = lax.broadcasted_iota(jnp.int32, (R, page), 1)
        valid = (s <= t) if diff else (s < t)
        attend(kpad_ref[...], vpad_ref[...], valid)

    for cp in page_copies(lin, slot):
        cp.wait()
    for r in range(G):
        attend(kbuf[slot, r], vbuf[slot, r], None)

    @pl.when(step == n_steps - 1)
    def _():
        acc = acc_ref[...]
        rg = lax.broadcasted_iota(jnp.int32, (R, D), 0)
        cg = lax.broadcasted_iota(jnp.int32, (R, D), 1) // out_w
        if diff:
            lam = _diff_lambda(lam_ref, lam_init)
            is_map1 = (lax.broadcasted_iota(jnp.int32, (R, 1), 0) // T) % 2 == 0
            coef = jnp.where(is_map1, 1.0, -lam) / s1_ref[...]
            acc = jnp.where(rg // (2 * T) == cg, acc * coef, 0.0)
        else:
            acc = jnp.where(rg // T == cg, acc, 0.0)
        n = R
        while n > 8:
            n //= 2
            acc = acc[:n] + acc[n:]
        assert 8 % T == 0
        sh = T
        while sh < 8:
            acc = acc + pltpu.roll(acc, sh, 0)
            sh *= 2
        if diff:
            g = g_ref[...]
            for h in range(D // out_w):
                sl = slice(h * out_w, (h + 1) * out_w)
                o_ref[:, sl] = _head_norm(acc[:, sl], g, lam_init).astype(o_ref.dtype)
        else:
            o_ref[...] = acc.astype(o_ref.dtype)


def _decode_attend(q_rep, k_new, v_new, cache_k, cache_v, page_table, layer, *, T, diff,
                   lam_par=None, g=None, lam_init=0.0, pages_per_step=4):
    Bd, R, D = q_rep.shape
    page = cache_k.shape[2]
    n_pages = page_table.shape[1]
    G = pages_per_step
    n_steps = n_pages // G
    npad = k_new.shape[1]

    row_spec = lambda n: pl.BlockSpec((None, n, D), lambda b, s, pt: (b, 0, 0))
    in_specs = [row_spec(R), row_spec(npad), row_spec(npad)]
    args = [q_rep, k_new, v_new]
    if diff:
        in_specs += [pl.BlockSpec(lam_par.shape, lambda b, s, pt: (0, 0)),
                     pl.BlockSpec(g.shape, lambda b, s, pt: (0, 0))]
        args += [lam_par, g]
    in_specs += [pl.BlockSpec(memory_space=pl.ANY)] * 2
    args += [cache_k, cache_v]
    kern = functools.partial(_decode_kernel, diff=diff, T=T, pages_per_step=G, page=page, layer=layer,
                             n_pages=n_pages, scale=(DIFF_HEAD_DIM if diff else SB_HEAD_DIM) ** -0.5,
                             lam_init=lam_init)
    return pl.pallas_call(
        kern,
        grid_spec=pltpu.PrefetchScalarGridSpec(
            num_scalar_prefetch=1,
            grid=(Bd, n_steps),
            in_specs=in_specs,
            out_specs=row_spec(npad),
            scratch_shapes=[
                pltpu.VMEM((R, D), BF16),
                pltpu.VMEM((R, D), F32),
                pltpu.VMEM((R, 1), F32),
                pltpu.VMEM((R, 1), F32),
                pltpu.VMEM((page, D), F32),
                pltpu.VMEM((page, D), F32),
                pltpu.VMEM((2, G, page, D), F32),
                pltpu.VMEM((2, G, page, D), F32),
                pltpu.SemaphoreType.DMA((2, 2)),
            ]),
        out_shape=jax.ShapeDtypeStruct((Bd, npad, D), F32),
        compiler_params=_cparams(("arbitrary", "arbitrary")),
        name="diff_decode" if diff else "sb_decode",
    )(page_table, *args)


def _layer_norm(x, g, b):
    mu = jnp.mean(x, axis=1, keepdims=True)
    xc = x - mu
    var = jnp.mean(xc * xc, axis=1, keepdims=True)
    return xc * lax.rsqrt(var + LN_EPS) * g + b


def _oproj_ln_kernel(o_ref, w_ref, h_ref, g_ref, b_ref, hf_ref, hb_ref, *, alpha):
    x = alpha * h_ref[...] + _dot(o_ref[...], w_ref[...])
    y = _layer_norm(x, g_ref[...], b_ref[...])
    hf_ref[...] = y
    hb_ref[...] = y.astype(BF16)


def _oproj_ln(o, w, h, g, b, *, tm, alpha):
    M, D = o.shape
    row = pl.BlockSpec((tm, D), lambda i: (i, 0))
    vec = pl.BlockSpec((1, D), lambda i: (0, 0))
    return pl.pallas_call(
        functools.partial(_oproj_ln_kernel, alpha=alpha),
        grid=(M // tm,),
        in_specs=[row, pl.BlockSpec((D, D), lambda i: (0, 0)), row, vec, vec],
        out_specs=[row, row],
        out_shape=[jax.ShapeDtypeStruct((M, D), F32), jax.ShapeDtypeStruct((M, D), BF16)],
        compiler_params=_cparams(("parallel",)),
        name="oproj_ln",
    )(o, w, h, g, b)


def _ffn_ln_kernel(x_ref, wg_ref, wu_ref, wd_ref, h_ref, g_ref, b_ref, hf_ref, hb_ref, acc_ref, *, alpha):
    j = pl.program_id(1)
    x = x_ref[...]
    gate = _dot(x, wg_ref[...])
    up = _dot(x, wu_ref[...])
    act = (gate * jax.nn.sigmoid(gate) * up).astype(BF16)
    part = _dot(act, wd_ref[...])

    @pl.when(j == 0)
    def _():
        acc_ref[...] = part

    @pl.when(j > 0)
    def _():
        acc_ref[...] += part

    @pl.when(j == pl.num_programs(1) - 1)
    def _():
        y = _layer_norm(alpha * h_ref[...] + acc_ref[...], g_ref[...], b_ref[...])
        hf_ref[...] = y
        hb_ref[...] = y.astype(BF16)


def _ffn_ln(x, wgu, wd, h, g, b, *, tm, tf, alpha):
    M, D = x.shape
    Fd = wd.shape[0]
    nf = Fd // tf
    row = pl.BlockSpec((tm, D), lambda i, j: (i, 0))
    vec = pl.BlockSpec((1, D), lambda i, j: (0, 0))
    return pl.pallas_call(
        functools.partial(_ffn_ln_kernel, alpha=alpha),
        grid=(M // tm, nf),
        in_specs=[
            row,
            pl.BlockSpec((D, tf), lambda i, j: (0, j)),
            pl.BlockSpec((D, tf), lambda i, j: (0, j + nf)),
            pl.BlockSpec((tf, D), lambda i, j: (j, 0)),
            row, vec, vec,
        ],
        out_specs=[row, row],
        out_shape=[jax.ShapeDtypeStruct((M, D), F32), jax.ShapeDtypeStruct((M, D), BF16)],
        scratch_shapes=[pltpu.VMEM((tm, D), F32)],
        compiler_params=_cparams(("parallel", "arbitrary")),
        name="ffn_ln",
    )(x, wgu, wgu, wd, h, g, b)


def _rope_tables(pos, d):
    half = d // 2
    inv = ROPE_THETA ** (-jnp.arange(half, dtype=F32) / half)
    ang = pos.astype(F32)[:, None] * inv[None, :]
    cos, sin = jnp.cos(ang), jnp.sin(ang)
    return jnp.concatenate([cos, cos], axis=1), jnp.concatenate([-sin, sin], axis=1)


def _pick(n, pref):
    for t in pref:
        if n % t == 0:
            return t
    return n


def kernel(x_prompt, x_sample, cache_sb_k, cache_sb_v, cache_diff_k, cache_diff_v, page_table,
           w_qkv, w_o, diff_lambda, diff_subln_g, ln_mix_g, ln_mix_b, w_gate_up, w_down,
           ln_ffn_g, ln_ffn_b):
    B, S, D = x_prompt.shape
    Bd, T, _ = x_sample.shape
    depth = w_qkv.shape[0]
    n_pages = page_table.shape[1]
    page = cache_sb_k.shape[2]
    P = n_pages * page
    Fd = w_down.shape[1]
    alpha = (2 * depth) ** 0.25
    Mp, Ms = B * S, Bd * T
    sb_heads = D // SB_HEAD_DIM
    diff_heads = D // (2 * DIFF_HEAD_DIM)

    tm_p = _pick(S, (512, 256, 128))
    tm_s = Ms
    tn = _pick(D, (512, 256, 128))
    blk = _pick(S, (256, 128))
    tm_o = _pick(S, (256, 128))
    tm_f = _pick(S, (512, 256, 128))
    tf = _pick(Fd, (512, 256, 128))
    npad = 8
    R = 16 * T

    wqkv_b = w_qkv.astype(BF16)
    wo_b = w_o.astype(BF16)
    wgu_b = w_gate_up.astype(BF16)
    wd_b = w_down.astype(BF16)

    tabs_p = _rope_tables(jnp.arange(S), DIFF_HEAD_DIM)
    tabs_s = _rope_tables(jnp.tile(P + jnp.arange(T), Bd), DIFF_HEAD_DIM)

    hf_p = x_prompt.reshape(Mp, D)
    hf_s = x_sample.reshape(Ms, D)
    hb_p = hf_p.astype(BF16)
    hb_s = hf_s.astype(BF16)

    def pad_rows(a):
        return jnp.pad(a.reshape(Bd, T, D), ((0, 0), (0, npad - T), (0, 0)))

    outs = {k: [] for k in ("sb_kp", "sb_vp", "sb_ks", "sb_vs", "df_kp", "df_vp", "df_ks", "df_vs")}
    for i in range(depth):
        j = i // N_MIXERS
        diff = i % N_MIXERS == 1
        vec = lambda a: a[i].reshape(1, D)
        q_p, kf_p, kb_p, vf_p, vb_p = _qkv_proj(hb_p, wqkv_b[i], tabs_p if diff else None,
                                                tm=tm_p, tn=tn, n_pos_blocks=S // tm_p)
        q_s, kf_s, _, vf_s, _ = _qkv_proj(hb_s, wqkv_b[i], tabs_s if diff else None,
                                          tm=tm_s, tn=tn, n_pos_blocks=1)
        q_rep = jnp.tile(q_s.reshape(Bd, 1, T, D), (1, R // T, 1, 1)).reshape(Bd, R, D)
        if diff:
            lam_init = 0.8 - 0.6 * math.exp(-0.3 * i)
            g_sub = diff_subln_g[j].reshape(1, 2 * DIFF_HEAD_DIM)
            o_p = _diff_attend_prompt(q_p, kb_p, vb_p, diff_lambda[j], g_sub, B=B, S=S, blk=blk,
                                      lam_init=lam_init)
            o_s = _decode_attend(q_rep, pad_rows(kf_s), pad_rows(vf_s), cache_diff_k, cache_diff_v, page_table, j,
                                 T=T, diff=True, lam_par=diff_lambda[j], g=g_sub, lam_init=lam_init)
            outs["df_kp"].append(kf_p.reshape(B, S, diff_heads, 2, DIFF_HEAD_DIM))
            outs["df_vp"].append(vf_p.reshape(B, S, diff_heads, 2 * DIFF_HEAD_DIM))
            outs["df_ks"].append(kf_s.reshape(Bd, T, diff_heads, 2, DIFF_HEAD_DIM))
            outs["df_vs"].append(vf_s.reshape(Bd, T, diff_heads, 2 * DIFF_HEAD_DIM))
        else:
            o_p = _sb_attend_prompt(q_p, kb_p, vb_p, B=B, S=S, blk=blk)
            o_s = _decode_attend(q_rep, pad_rows(kf_s), pad_rows(vf_s), cache_sb_k, cache_sb_v, page_table, j,
                                 T=T, diff=False)
            outs["sb_kp"].append(kf_p.reshape(B, S, sb_heads, SB_HEAD_DIM))
            outs["sb_vp"].append(vf_p.reshape(B, S, sb_heads, SB_HEAD_DIM))
            outs["sb_ks"].append(kf_s.reshape(Bd, T, sb_heads, SB_HEAD_DIM))
            outs["sb_vs"].append(vf_s.reshape(Bd, T, sb_heads, SB_HEAD_DIM))
        o_s = o_s[:, :T].reshape(Ms, D).astype(BF16)
        hf_p, hb_p = _oproj_ln(o_p, wo_b[i], hf_p, vec(ln_mix_g), vec(ln_mix_b), tm=tm_o, alpha=alpha)
        hf_s, hb_s = _oproj_ln(o_s, wo_b[i], hf_s, vec(ln_mix_g), vec(ln_mix_b), tm=tm_s, alpha=alpha)
        hf_p, hb_p = _ffn_ln(hb_p, wgu_b[i], wd_b[i], hf_p, vec(ln_ffn_g), vec(ln_ffn_b),
                             tm=tm_f, tf=tf, alpha=alpha)
        hf_s, hb_s = _ffn_ln(hb_s, wgu_b[i], wd_b[i], hf_s, vec(ln_ffn_g), vec(ln_ffn_b),
                             tm=tm_s, tf=tf, alpha=alpha)

    st = lambda k: jnp.stack(outs[k])
    return (hf_p.reshape(B, S, D), hf_s.reshape(Bd, T, D),
            st("sb_kp"), st("sb_vp"), st("df_kp"), st("df_vp"),
            st("sb_ks"), st("sb_vs"), st("df_ks"), st("df_vs"))
```

```python
import functools
import math

import jax
import jax.numpy as jnp
from jax import lax
from jax.experimental import pallas as pl
from jax.experimental.pallas import tpu as pltpu

F32 = jnp.float32
BF16 = jnp.bfloat16

ROPE_THETA = 10000.0
LN_EPS = 1e-5
SUBLN_EPS = 1e-5
SB_HEAD_DIM = 128
DIFF_HEAD_DIM = 128
N_MIXERS = 2

LANES = 128
SUBLANES = 8
VMEM_LIMIT_BYTES = 56 * 1024 * 1024

LOG2E = math.log2(math.e)
PV_SKEW = 2
NEG_BIG = -1e30


def _cparams(sem, flags=None):
    return pltpu.CompilerParams(dimension_semantics=sem, vmem_limit_bytes=VMEM_LIMIT_BYTES, flags=flags)


_ATTN_FLAGS = None


def _dot(a, b):
    return jnp.dot(a, b, preferred_element_type=F32)


def _dot_nt(a, b):
    return lax.dot_general(a, b, (((1,), (1,)), ((), ())), preferred_element_type=F32)


def _rope_cols(a, cos, sin):
    return a * cos + pltpu.roll(a, LANES // 2, 1) * sin


def _qkv_kernel(*refs, rope, tn):
    if rope:
        x_ref, wq_ref, wk_ref, wv_ref, cos_ref, sin_ref, q_ref, kf_ref, kb_ref, vf_ref, vb_ref = refs
    else:
        x_ref, wq_ref, wk_ref, wv_ref, q_ref, kf_ref, kb_ref, vf_ref, vb_ref = refs
    x = x_ref[...]
    q = _dot(x, wq_ref[...])
    k = _dot(x, wk_ref[...])
    v = _dot(x, wv_ref[...])
    vf_ref[...] = v
    vb_ref[...] = v.astype(BF16)
    if rope:
        cos = cos_ref[...]
        sin = sin_ref[...]
        for g in range(tn // LANES):
            sl = slice(g * LANES, (g + 1) * LANES)
            qg = _rope_cols(q[:, sl], cos, sin)
            kg = _rope_cols(k[:, sl], cos, sin)
            q_ref[:, sl] = qg.astype(BF16)
            kf_ref[:, sl] = kg
            kb_ref[:, sl] = kg.astype(BF16)
    else:
        q_ref[...] = q.astype(BF16)
        kf_ref[...] = k
        kb_ref[...] = k.astype(BF16)


def _qkv_proj(x, w, layer, rope_tabs, *, tm, tn, n_pos_blocks):
    M, D = x.shape
    nj = D // tn
    rope = rope_tabs is not None
    in_specs = [
        pl.BlockSpec((tm, D), lambda i, j: (i, 0)),
        pl.BlockSpec((None, D, tn), lambda i, j: (layer, 0, j)),
        pl.BlockSpec((None, D, tn), lambda i, j: (layer, 0, j + nj)),
        pl.BlockSpec((None, D, tn), lambda i, j: (layer, 0, j + 2 * nj)),
    ]
    args = [x, w, w, w]
    if rope:
        tab_spec = pl.BlockSpec((tm, LANES), lambda i, j: (i % n_pos_blocks, 0))
        in_specs += [tab_spec, tab_spec]
        args += list(rope_tabs)
    out_spec = pl.BlockSpec((tm, tn), lambda i, j: (i, j))
    return pl.pallas_call(
        functools.partial(_qkv_kernel, rope=rope, tn=tn),
        grid=(M // tm, nj),
        in_specs=in_specs,
        out_specs=[out_spec] * 5,
        out_shape=[jax.ShapeDtypeStruct((M, D), dt) for dt in (BF16, F32, BF16, F32, BF16)],
        compiler_params=_cparams(("parallel", "arbitrary")),
        name="qkv_rope" if rope else "qkv",
    )(*args)


def _suffix_matrix(n):
    row = lax.broadcasted_iota(jnp.int32, (n, n), 0)
    col = lax.broadcasted_iota(jnp.int32, (n, n), 1)
    return jnp.where(row > col, 1.0, 0.0).astype(BF16)


def _suffix_sum(x, u):
    hi = x.astype(BF16)
    lo = (x - hi.astype(F32)).astype(BF16)
    return _dot(hi, u) + _dot(lo, u)


def _neg_abs(x):
    return pltpu.bitcast(pltpu.bitcast(x, jnp.uint32) | jnp.uint32(0x80000000), F32)


def _sb_logs(w, valid):
    l1p = jnp.log2(1.0 + jnp.exp2(_neg_abs(w)))
    log_sig = jnp.minimum(w, 0.0) - l1p
    log_1m = log_sig - w
    if valid is not None:
        log_1m = jnp.where(valid, log_1m, 0.0)
    return log_sig, log_1m


def _lanes(x, n):
    reps = n // LANES
    return x if reps == 1 or x.shape[1] == 1 else jnp.concatenate([x] * reps, axis=1)


def _sb_weights(log_sig, tail, carry, valid):
    a = jnp.exp2(log_sig + tail + _lanes(carry, log_sig.shape[1]))
    if valid is not None:
        a = jnp.where(valid, a, 0.0)
    return a


def _sb_prompt_kernel(q_ref, k_ref, v_ref, o_ref, carry_ref, acc_ref, *, blk, heads, scale2):
    i = pl.program_id(2)
    d = SB_HEAD_DIM
    u = _suffix_matrix(blk)
    row = lax.broadcasted_iota(jnp.int32, (blk, blk), 0)
    col = lax.broadcasted_iota(jnp.int32, (blk, blk), 1)
    carry_ref[...] = jnp.zeros_like(carry_ref)
    acc_ref[...] = jnp.zeros_like(acc_ref)

    def tile(j, valid):
        start = pl.multiple_of(j * blk, blk)
        hss = [slice(h * d, (h + 1) * d) for h in range(heads)]
        ws = [_dot_nt(q_ref[:, hs], k_ref[pl.ds(start, blk), hs]) for hs in hss]
        logs = [_sb_logs(w * scale2, valid) for w in ws]
        tails = [_suffix_sum(log_1m, u) for _, log_1m in logs]
        probs = []
        for h in range(heads):
            log_sig, log_1m = logs[h]
            probs.append(_sb_weights(log_sig, tails[h], carry_ref[h], valid).astype(BF16))
            carry_ref[h] += jnp.sum(log_1m, axis=1, keepdims=True)
        for h, hs in enumerate(hss):
            acc_ref[:, hs] += _dot(probs[h], v_ref[pl.ds(start, blk), hs])

    tile(i, col < row)

    def body(jj, c):
        tile(i - 1 - jj, None)
        return c

    lax.fori_loop(0, i, body, 0)
    o_ref[...] = acc_ref[...].astype(o_ref.dtype)


def _sb_attend_prompt(q, k, v, *, B, S, blk, heads):
    M, D = q.shape
    wblk = heads * SB_HEAD_DIM
    nq = S // blk
    qspec = pl.BlockSpec((blk, wblk), lambda b, h, i: (b * nq + i, h))
    kvspec = pl.BlockSpec((S, wblk), lambda b, h, i: (b, h))
    return pl.pallas_call(
        functools.partial(_sb_prompt_kernel, blk=blk, heads=heads, scale2=SB_HEAD_DIM ** -0.5 * LOG2E),
        grid=(B, D // wblk, nq),
        in_specs=[qspec, kvspec, kvspec],
        out_specs=qspec,
        out_shape=jax.ShapeDtypeStruct((M, D), BF16),
        scratch_shapes=[pltpu.VMEM((heads, blk, 1), F32),
                        pltpu.VMEM((blk, wblk), F32)],
        compiler_params=_cparams(("parallel", "parallel", "arbitrary"), _ATTN_FLAGS),
        name="sb_prompt",
    )(q, k, v)


def _diff_lambda(lam_ref, lam_init):
    lv = lam_ref[...]
    s1 = jnp.sum(lv[0:1] * lv[1:2], axis=1, keepdims=True)
    s2 = jnp.sum(lv[2:3] * lv[3:4], axis=1, keepdims=True)
    return jnp.exp(s1) - jnp.exp(s2) + lam_init


def _softmax_step(w, m, l):
    m_new = jnp.maximum(m, jnp.max(w, axis=1, keepdims=True))
    alpha = jnp.exp2(m - m_new)
    p = jnp.exp2(w - _lanes(m_new, w.shape[1]))
    return p, alpha, m_new, alpha * l + jnp.sum(p, axis=1, keepdims=True)


def _head_norm(o, g, lam_init):
    ms = jnp.mean(o * o, axis=1, keepdims=True)
    return o * lax.rsqrt(ms + SUBLN_EPS) * g * (1.0 - lam_init)


def _diff_prompt_kernel(q_ref, k_ref, v_ref, lam_ref, g_ref, o_ref, m_ref, l_ref, acc_ref, *,
                        blk, heads, scale2, lam_init):
    i = pl.program_id(2)
    d = DIFF_HEAD_DIM
    row = lax.broadcasted_iota(jnp.int32, (blk, blk), 0)
    col = lax.broadcasted_iota(jnp.int32, (blk, blk), 1)
    m_ref[...] = jnp.full(m_ref.shape, NEG_BIG, F32)
    l_ref[...] = jnp.zeros_like(l_ref)
    acc_ref[...] = jnp.zeros_like(acc_ref)

    def tile(j, valid):
        start = pl.multiple_of(j * blk, blk)
        chains = range(2 * heads)
        css = [slice(c * d, (c + 1) * d) for c in chains]
        probs, alphas = {}, {}
        for t in range(len(chains) + PV_SKEW):
            if t < len(chains):
                c = t
                w = _dot_nt(q_ref[:, css[c]], k_ref[pl.ds(start, blk), css[c]]) * scale2
                if valid is not None:
                    w = jnp.where(valid, w, NEG_BIG)
                p, al, m_new, l_new = _softmax_step(w, m_ref[c], l_ref[c])
                m_ref[c] = m_new
                l_ref[c] = l_new
                probs[c] = p.astype(BF16)
                alphas[c] = al
            c = t - PV_SKEW
            if 0 <= c < len(chains):
                h = c // 2
                pv = _dot(probs.pop(c), v_ref[pl.ds(start, blk), h * 2 * d:(h + 1) * 2 * d])
                acc_ref[c] = _lanes(alphas.pop(c), 2 * d) * acc_ref[c] + pv

    tile(i, col <= row)

    def body(jj, c):
        tile(i - 1 - jj, None)
        return c

    lax.fori_loop(0, i, body, 0)
    lam = _diff_lambda(lam_ref, lam_init)
    g = g_ref[...]
    for h in range(heads):
        o = (acc_ref[2 * h] / _lanes(l_ref[2 * h], 2 * d)
             - lam * (acc_ref[2 * h + 1] / _lanes(l_ref[2 * h + 1], 2 * d)))
        o_ref[:, h * 2 * d:(h + 1) * 2 * d] = _head_norm(o, g, lam_init).astype(o_ref.dtype)


def _diff_attend_prompt(q, k, v, lam_par, g, *, B, S, blk, heads, lam_init):
    M, D = q.shape
    wblk = heads * 2 * DIFF_HEAD_DIM
    nq = S // blk
    qspec = pl.BlockSpec((blk, wblk), lambda b, h, i: (b * nq + i, h))
    kvspec = pl.BlockSpec((S, wblk), lambda b, h, i: (b, h))
    return pl.pallas_call(
        functools.partial(_diff_prompt_kernel, blk=blk, heads=heads,
                          scale2=DIFF_HEAD_DIM ** -0.5 * LOG2E, lam_init=lam_init),
        grid=(B, D // wblk, nq),
        in_specs=[qspec, kvspec, kvspec,
                  pl.BlockSpec(lam_par.shape, lambda b, h, i: (0, 0)),
                  pl.BlockSpec(g.shape, lambda b, h, i: (0, 0))],
        out_specs=qspec,
        out_shape=jax.ShapeDtypeStruct((M, D), BF16),
        scratch_shapes=[pltpu.VMEM((2 * heads, blk, LANES), F32),
                        pltpu.VMEM((2 * heads, blk, LANES), F32),
                        pltpu.VMEM((2 * heads, blk, 2 * DIFF_HEAD_DIM), F32)],
        compiler_params=_cparams(("parallel", "parallel", "arbitrary"), _ATTN_FLAGS),
        name="diff_prompt",
    )(q, k, v, lam_par, g)


def _head_views(cache_ref, layer, pid):
    if len(cache_ref.shape) == 5:
        return [cache_ref.at[layer, pid, :, h, :] for h in range(cache_ref.shape[3])]
    return [cache_ref.at[layer, pid, :, h, m, :]
            for h in range(cache_ref.shape[3]) for m in range(cache_ref.shape[4])]


def _decode_kernel(*refs, diff, T, pages_per_step, page, layer, n_pages, scale2, lam_init):
    G = pages_per_step
    if diff:
        (pt_ref, q_ref, kn_ref, vn_ref, lam_ref, g_ref), rest = refs[:6], refs[6:]
    else:
        (pt_ref, q_ref, kn_ref, vn_ref), rest = refs[:4], refs[4:]
    ck_ref, cv_ref, o_ref, qbd_ref, acc_ref, s0_ref, s1_ref, kpad_ref, vpad_ref, kbuf, vbuf, sem = rest
    step = pl.program_id(1)
    n_steps = n_pages // G
    total = pl.num_programs(0) * n_steps
    lin = pl.program_id(0) * n_steps + step
    slot = lax.rem(lin, 2)
    R, D = qbd_ref.shape
    slot_w = D // (R // T)
    out_w = 2 * slot_w if diff else slot_w
    u = None if diff else _suffix_matrix(page)

    def page_copies(lin_step, sl):
        bb = lax.div(lin_step, n_steps)
        first = n_pages - (lax.rem(lin_step, n_steps) + 1) * G
        cps = []
        for c in range(G):
            pid = pt_ref[bb, first + c]
            for c_ref, buf, si in ((ck_ref, kbuf, 0), (cv_ref, vbuf, 1)):
                views = _head_views(c_ref, layer, pid)
                w = D // len(views)
                for n, src in enumerate(views):
                    dst = buf.at[sl, pl.ds(c * page, page), pl.ds(n * w, w)]
                    cps.append(pltpu.make_async_copy(src, dst, sem.at[si, sl]))
        return cps

    @pl.when(lin == 0)
    def _():
        for cp in page_copies(lin, slot):
            cp.start()

    for cp in page_copies(lax.rem(lin + 1, total), 1 - slot):
        cp.start()

    def attend(kf, vf, valid):
        n = kf.shape[0]
        w = _dot_nt(qbd_ref[...], kf.astype(BF16)) * scale2
        vb = vf.astype(BF16)
        if diff:
            if valid is not None:
                w = jnp.where(valid, w, NEG_BIG)
            p, alpha, m_new, l_new = _softmax_step(w, s0_ref[...], s1_ref[...])
            s0_ref[...] = m_new
            s1_ref[...] = l_new
            acc_ref[...] = _lanes(alpha, D) * acc_ref[...] + _dot(p.astype(BF16), vb)
        else:
            log_sig, log_1m = _sb_logs(w, valid)
            chunks = [slice(c * page, (c + 1) * page) for c in range(n // page)]
            carry = s0_ref[...]
            parts = [None] * len(chunks)
            for c in reversed(range(len(chunks))):
                sl = chunks[c]
                tail = _suffix_sum(log_1m[:, sl], u)
                parts[c] = _sb_weights(log_sig[:, sl], tail, carry, None if valid is None else valid[:, sl])
                carry = carry + jnp.sum(log_1m[:, sl], axis=1, keepdims=True)
            s0_ref[...] = carry
            a = parts[0] if len(parts) == 1 else jnp.concatenate(parts, axis=1)
            acc_ref[...] += _dot(a.astype(BF16), vb)

    @pl.when(step == 0)
    def _():
        rg = lax.broadcasted_iota(jnp.int32, (R, D), 0) // T
        cg = lax.broadcasted_iota(jnp.int32, (R, D), 1) // slot_w
        qbd_ref[...] = jnp.where(rg == cg, q_ref[...], jnp.zeros((), BF16))
        acc_ref[...] = jnp.zeros_like(acc_ref)
        s0_ref[...] = jnp.full(s0_ref.shape, NEG_BIG if diff else 0.0, F32)
        s1_ref[...] = jnp.zeros_like(s1_ref)
        kpad_ref[...] = jnp.zeros_like(kpad_ref)
        vpad_ref[...] = jnp.zeros_like(vpad_ref)
        kpad_ref[0:kn_ref.shape[0], :] = kn_ref[...]
        vpad_ref[0:vn_ref.shape[0], :] = vn_ref[...]
        t = lax.broadcasted_iota(jnp.int32, (R, page), 0) % T
        s = lax.broadcasted_iota(jnp.int32, (R, page), 1)
        valid = (s <= t) if diff else (s < t)
        attend(kpad_ref[...], vpad_ref[...], valid)

    for cp in page_copies(lin, slot):
        cp.wait()
    attend(kbuf[slot], vbuf[slot], None)

    @pl.when(lin == total - 1)
    def _():
        for cp in page_copies(lin * 0, 1 - slot):
            cp.wait()

    @pl.when(step == n_steps - 1)
    def _():
        acc = acc_ref[...]
        rg = lax.broadcasted_iota(jnp.int32, (R, D), 0)
        cg = lax.broadcasted_iota(jnp.int32, (R, D), 1) // out_w
        if diff:
            lam = _diff_lambda(lam_ref, lam_init)
            is_map1 = (lax.broadcasted_iota(jnp.int32, (R, LANES), 0) // T) % 2 == 0
            coef = jnp.where(is_map1, 1.0, -lam) / s1_ref[...]
            acc = jnp.where(rg // (2 * T) == cg, acc * _lanes(coef, D), 0.0)
        else:
            acc = jnp.where(rg // T == cg, acc, 0.0)
        n = R
        while n > SUBLANES:
            n //= 2
            acc = acc[:n] + acc[n:]
        assert SUBLANES % T == 0
        sh = T
        while sh < SUBLANES:
            acc = acc + pltpu.roll(acc, sh, 0)
            sh *= 2
        if diff:
            g = g_ref[...]
            for h in range(D // out_w):
                sl = slice(h * out_w, (h + 1) * out_w)
                o_ref[:, sl] = _head_norm(acc[:, sl], g, lam_init).astype(o_ref.dtype)
        else:
            o_ref[...] = acc.astype(o_ref.dtype)


def _decode_attend(q_rep, k_new, v_new, cache_k, cache_v, page_table, layer, *, T, diff,
                   lam_par=None, g=None, lam_init=0.0, pages_per_step):
    Bd, R, D = q_rep.shape
    page = cache_k.shape[2]
    n_pages = page_table.shape[1]
    G = pages_per_step
    n_steps = n_pages // G
    npad = k_new.shape[1]

    row_spec = lambda n: pl.BlockSpec((None, n, D), lambda b, s, pt: (b, 0, 0))
    in_specs = [row_spec(R), row_spec(npad), row_spec(npad)]
    args = [q_rep, k_new, v_new]
    if diff:
        in_specs += [pl.BlockSpec(lam_par.shape, lambda b, s, pt: (0, 0)),
                     pl.BlockSpec(g.shape, lambda b, s, pt: (0, 0))]
        args += [lam_par, g]
    in_specs += [pl.BlockSpec(memory_space=pl.ANY)] * 2
    args += [cache_k, cache_v]
    kern = functools.partial(_decode_kernel, diff=diff, T=T, pages_per_step=G, page=page, layer=layer,
                             n_pages=n_pages, scale2=(DIFF_HEAD_DIM if diff else SB_HEAD_DIM) ** -0.5 * LOG2E,
                             lam_init=lam_init)
    return pl.pallas_call(
        kern,
        grid_spec=pltpu.PrefetchScalarGridSpec(
            num_scalar_prefetch=1,
            grid=(Bd, n_steps),
            in_specs=in_specs,
            out_specs=row_spec(npad),
            scratch_shapes=[
                pltpu.VMEM((R, D), BF16),
                pltpu.VMEM((R, D), F32),
                pltpu.VMEM((R, LANES), F32),
                pltpu.VMEM((R, LANES), F32),
                pltpu.VMEM((page, D), F32),
                pltpu.VMEM((page, D), F32),
                pltpu.VMEM((2, G * page, D), F32),
                pltpu.VMEM((2, G * page, D), F32),
                pltpu.SemaphoreType.DMA((2, 2)),
            ]),
        out_shape=jax.ShapeDtypeStruct((Bd, npad, D), F32),
        compiler_params=_cparams(("arbitrary", "arbitrary")),
        name="diff_decode" if diff else "sb_decode",
    )(page_table, *args)


def _layer_norm(x, g, b):
    mu = jnp.mean(x, axis=1, keepdims=True)
    xc = x - mu
    var = jnp.mean(xc * xc, axis=1, keepdims=True)
    return xc * lax.rsqrt(var + LN_EPS) * g + b


def _oproj_ln_kernel(o_ref, w_ref, h_ref, g_ref, b_ref, hf_ref, hb_ref, *, alpha):
    x = alpha * h_ref[...] + _dot(o_ref[...], w_ref[...])
    y = _layer_norm(x, g_ref[...], b_ref[...])
    hf_ref[...] = y
    hb_ref[...] = y.astype(BF16)


def _oproj_ln(o, w, layer, h, g, b, *, tm, alpha):
    M, D = o.shape
    row = pl.BlockSpec((tm, D), lambda i: (i, 0))
    vec = pl.BlockSpec((None, 1, D), lambda i: (layer, 0, 0))
    return pl.pallas_call(
        functools.partial(_oproj_ln_kernel, alpha=alpha),
        grid=(M // tm,),
        in_specs=[row, pl.BlockSpec((None, D, D), lambda i: (layer, 0, 0)), row, vec, vec],
        out_specs=[row, row],
        out_shape=[jax.ShapeDtypeStruct((M, D), F32), jax.ShapeDtypeStruct((M, D), BF16)],
        compiler_params=_cparams(("parallel",)),
        name="oproj_ln",
    )(o, w, h, g, b)


def _ffn_ln_kernel(x_ref, wg_ref, wu_ref, wd_ref, h_ref, g_ref, b_ref, hf_ref, hb_ref, acc_ref, *, alpha):
    j = pl.program_id(1)

    @pl.when(j == 0)
    def _():
        acc_ref[...] = jnp.zeros_like(acc_ref)

    x = x_ref[...]
    gate = _dot(x, wg_ref[...])
    up = _dot(x, wu_ref[...])
    act = (gate * jax.nn.sigmoid(gate) * up).astype(BF16)
    acc_ref[...] += _dot(act, wd_ref[...])

    @pl.when(j == pl.num_programs(1) - 1)
    def _():
        y = _layer_norm(alpha * h_ref[...] + acc_ref[...], g_ref[...], b_ref[...])
        hf_ref[...] = y
        hb_ref[...] = y.astype(BF16)


def _ffn_ln(x, wgu, wd, layer, h, g, b, *, tm, tf, alpha):
    M, D = x.shape
    Fd = wd.shape[1]
    nf = Fd // tf
    row = pl.BlockSpec((tm, D), lambda i, j: (i, 0))
    vec = pl.BlockSpec((None, 1, D), lambda i, j: (layer, 0, 0))
    return pl.pallas_call(
        functools.partial(_ffn_ln_kernel, alpha=alpha),
        grid=(M // tm, nf),
        in_specs=[
            row,
            pl.BlockSpec((None, D, tf), lambda i, j: (layer, 0, j)),
            pl.BlockSpec((None, D, tf), lambda i, j: (layer, 0, j + nf)),
            pl.BlockSpec((None, tf, D), lambda i, j: (layer, j, 0)),
            row, vec, vec,
        ],
        out_specs=[row, row],
        out_shape=[jax.ShapeDtypeStruct((M, D), F32), jax.ShapeDtypeStruct((M, D), BF16)],
        scratch_shapes=[pltpu.VMEM((tm, D), F32)],
        compiler_params=_cparams(("parallel", "arbitrary")),
        name="ffn_ln",
    )(x, wgu, wgu, wd, h, g, b)


def _rope_tables(pos, d):
    half = d // 2
    inv = ROPE_THETA ** (-jnp.arange(half, dtype=F32) / half)
    ang = pos.astype(F32)[:, None] * inv[None, :]
    cos, sin = jnp.cos(ang), jnp.sin(ang)
    return jnp.concatenate([cos, cos], axis=1), jnp.concatenate([-sin, sin], axis=1)


def _pick(n, pref):
    for t in pref:
        if n % t == 0:
            return t
    return n


def kernel(x_prompt, x_sample, cache_sb_k, cache_sb_v, cache_diff_k, cache_diff_v, page_table,
           w_qkv, w_o, diff_lambda, diff_subln_g, ln_mix_g, ln_mix_b, w_gate_up, w_down,
           ln_ffn_g, ln_ffn_b):
    B, S, D = x_prompt.shape
    Bd, T, _ = x_sample.shape
    depth = w_qkv.shape[0]
    n_pages = page_table.shape[1]
    page = cache_sb_k.shape[2]
    P = n_pages * page
    Fd = w_down.shape[1]
    alpha = (2 * depth) ** 0.25
    Mp, Ms = B * S, Bd * T
    sb_heads = D // SB_HEAD_DIM
    diff_heads = D // (2 * DIFF_HEAD_DIM)

    tm_p = _pick(S, (512, 256, 128))
    tm_s = Ms
    tn = _pick(D, (512, 256, 128))
    blk = _pick(S, (256, 128))
    tm_o = _pick(S, (256, 128))
    tm_f = _pick(S, (512, 256, 128))
    tf = _pick(Fd, (512, 256, 128))
    pages_per_step = _pick(n_pages, (8, 4, 2, 1))
    npad = SUBLANES
    R = 16 * T

    wqkv_b = w_qkv.astype(BF16)
    wo_b = w_o.astype(BF16)
    wgu_b = w_gate_up.astype(BF16)
    wd_b = w_down.astype(BF16)
    vecs = lambda a: a.reshape(depth, 1, D)
    mix_g, mix_b, ffn_g, ffn_b = vecs(ln_mix_g), vecs(ln_mix_b), vecs(ln_ffn_g), vecs(ln_ffn_b)

    tabs_p = _rope_tables(jnp.arange(S), DIFF_HEAD_DIM)
    tabs_s = _rope_tables(jnp.tile(P + jnp.arange(T), Bd), DIFF_HEAD_DIM)

    hf_p = x_prompt.reshape(Mp, D)
    hf_s = x_sample.reshape(Ms, D)
    hb_p = hf_p.astype(BF16)
    hb_s = hf_s.astype(BF16)

    def pad_rows(a):
        return jnp.pad(a.reshape(Bd, T, D), ((0, 0), (0, npad - T), (0, 0)))

    outs = {k: [] for k in ("sb_kp", "sb_vp", "sb_ks", "sb_vs", "df_kp", "df_vp", "df_ks", "df_vs")}
    for i in range(depth):
        j = i // N_MIXERS
        diff = i % N_MIXERS == 1
        q_p, kf_p, kb_p, vf_p, vb_p = _qkv_proj(hb_p, wqkv_b, i, tabs_p if diff else None,
                                                tm=tm_p, tn=tn, n_pos_blocks=S // tm_p)
        q_s, kf_s, _, vf_s, _ = _qkv_proj(hb_s, wqkv_b, i, tabs_s if diff else None,
                                          tm=tm_s, tn=tn, n_pos_blocks=1)
        q_rep = jnp.tile(q_s.reshape(Bd, 1, T, D), (1, R // T, 1, 1)).reshape(Bd, R, D)
        if diff:
            lam_init = 0.8 - 0.6 * math.exp(-0.3 * i)
            g_sub = diff_subln_g[j].reshape(1, 2 * DIFF_HEAD_DIM)
            o_p = _diff_attend_prompt(q_p, kb_p, vb_p, diff_lambda[j], g_sub, B=B, S=S, blk=blk,
                                      heads=4, lam_init=lam_init)
            o_s = _decode_attend(q_rep, pad_rows(kf_s), pad_rows(vf_s), cache_diff_k, cache_diff_v,
                                 page_table, j, T=T, diff=True, lam_par=diff_lambda[j], g=g_sub,
                                 lam_init=lam_init, pages_per_step=pages_per_step)
            outs["df_kp"].append(kf_p.reshape(B, S, diff_heads, 2, DIFF_HEAD_DIM))
            outs["df_vp"].append(vf_p.reshape(B, S, diff_heads, 2 * DIFF_HEAD_DIM))
            outs["df_ks"].append(kf_s.reshape(Bd, T, diff_heads, 2, DIFF_HEAD_DIM))
            outs["df_vs"].append(vf_s.reshape(Bd, T, diff_heads, 2 * DIFF_HEAD_DIM))
        else:
            o_p = _sb_attend_prompt(q_p, kb_p, vb_p, B=B, S=S, blk=blk, heads=4)
            o_s = _decode_attend(q_rep, pad_rows(kf_s), pad_rows(vf_s), cache_sb_k, cache_sb_v,
                                 page_table, j, T=T, diff=False, pages_per_step=pages_per_step)
            outs["sb_kp"].append(kf_p.reshape(B, S, sb_heads, SB_HEAD_DIM))
            outs["sb_vp"].append(vf_p.reshape(B, S, sb_heads, SB_HEAD_DIM))
            outs["sb_ks"].append(kf_s.reshape(Bd, T, sb_heads, SB_HEAD_DIM))
            outs["sb_vs"].append(vf_s.reshape(Bd, T, sb_heads, SB_HEAD_DIM))
        o_s = o_s[:, :T].reshape(Ms, D).astype(BF16)
        hf_p, hb_p = _oproj_ln(o_p, wo_b, i, hf_p, mix_g, mix_b, tm=tm_o, alpha=alpha)
        hf_s, hb_s = _oproj_ln(o_s, wo_b, i, hf_s, mix_g, mix_b, tm=tm_s, alpha=alpha)
        hf_p, hb_p = _ffn_ln(hb_p, wgu_b, wd_b, i, hf_p, ffn_g, ffn_b, tm=tm_f, tf=tf, alpha=alpha)
        hf_s, hb_s = _ffn_ln(hb_s, wgu_b, wd_b, i, hf_s, ffn_g, ffn_b, tm=tm_s, tf=tf, alpha=alpha)

    st = lambda k: jnp.stack(outs[k])
    return (hf_p.reshape(B, S, D), hf_s.reshape(Bd, T, D),
            st("sb_kp"), st("sb_vp"), st("df_kp"), st("df_vp"),
            st("sb_ks"), st("sb_vs"), st("df_ks"), st("df_vs"))
```

```python
import functools
import math

import jax
import jax.numpy as jnp
from jax import lax
from jax.experimental import pallas as pl
from jax.experimental.pallas import tpu as pltpu

F32 = jnp.float32
BF16 = jnp.bfloat16

ROPE_THETA = 10000.0
LN_EPS = 1e-5
SUBLN_EPS = 1e-5
SB_HEAD_DIM = 128
DIFF_HEAD_DIM = 128
N_MIXERS = 2

LANES = 128
SUBLANES = 8
VMEM_LIMIT_BYTES = 56 * 1024 * 1024

LOG2E = math.log2(math.e)
PV_SKEW = 2
NEG_BIG = -1e30


def _cparams(sem, flags=None):
    return pltpu.CompilerParams(dimension_semantics=sem, vmem_limit_bytes=VMEM_LIMIT_BYTES, flags=flags)


_ATTN_FLAGS = None


def _dot(a, b):
    return jnp.dot(a, b, preferred_element_type=F32)


def _dot_nt(a, b):
    return lax.dot_general(a, b, (((1,), (1,)), ((), ())), preferred_element_type=F32)


def _rope_cols(a, cos, sin):
    return a * cos + pltpu.roll(a, LANES // 2, 1) * sin


def _qkv_kernel(*refs, rope, tn):
    if rope:
        x_ref, wq_ref, wk_ref, wv_ref, cos_ref, sin_ref, q_ref, kf_ref, kb_ref, vf_ref, vb_ref = refs
    else:
        x_ref, wq_ref, wk_ref, wv_ref, q_ref, kf_ref, kb_ref, vf_ref, vb_ref = refs
    x = x_ref[...]
    q = _dot(x, wq_ref[...])
    k = _dot(x, wk_ref[...])
    v = _dot(x, wv_ref[...])
    vf_ref[...] = v
    vb_ref[...] = v.astype(BF16)
    if rope:
        cos = cos_ref[...]
        sin = sin_ref[...]
        for g in range(tn // LANES):
            sl = slice(g * LANES, (g + 1) * LANES)
            qg = _rope_cols(q[:, sl], cos, sin)
            kg = _rope_cols(k[:, sl], cos, sin)
            q_ref[:, sl] = qg.astype(BF16)
            kf_ref[:, sl] = kg
            kb_ref[:, sl] = kg.astype(BF16)
    else:
        q_ref[...] = q.astype(BF16)
        kf_ref[...] = k
        kb_ref[...] = k.astype(BF16)


def _qkv_proj(x, w, layer, rope_tabs, *, tm, tn, n_pos_blocks):
    M, D = x.shape
    nj = D // tn
    rope = rope_tabs is not None
    in_specs = [
        pl.BlockSpec((tm, D), lambda i, j: (i, 0)),
        pl.BlockSpec((None, D, tn), lambda i, j: (layer, 0, j)),
        pl.BlockSpec((None, D, tn), lambda i, j: (layer, 0, j + nj)),
        pl.BlockSpec((None, D, tn), lambda i, j: (layer, 0, j + 2 * nj)),
    ]
    args = [x, w, w, w]
    if rope:
        tab_spec = pl.BlockSpec((tm, LANES), lambda i, j: (i % n_pos_blocks, 0))
        in_specs += [tab_spec, tab_spec]
        args += list(rope_tabs)
    out_spec = pl.BlockSpec((tm, tn), lambda i, j: (i, j))
    return pl.pallas_call(
        functools.partial(_qkv_kernel, rope=rope, tn=tn),
        grid=(M // tm, nj),
        in_specs=in_specs,
        out_specs=[out_spec] * 5,
        out_shape=[jax.ShapeDtypeStruct((M, D), dt) for dt in (BF16, F32, BF16, F32, BF16)],
        compiler_params=_cparams(("parallel", "arbitrary")),
        name="qkv_rope" if rope else "qkv",
    )(*args)


def _suffix_matrix(n):
    row = lax.broadcasted_iota(jnp.int32, (n, n), 0)
    col = lax.broadcasted_iota(jnp.int32, (n, n), 1)
    return jnp.where(row > col, 1.0, 0.0).astype(BF16)


def _suffix_sum(x, u):
    hi = x.astype(BF16)
    lo = (x - hi.astype(F32)).astype(BF16)
    return _dot(hi, u) + _dot(lo, u)


def _neg_abs(x):
    return pltpu.bitcast(pltpu.bitcast(x, jnp.uint32) | jnp.uint32(0x80000000), F32)


def _sb_logs(w, valid):
    l1p = jnp.log2(1.0 + jnp.exp2(_neg_abs(w)))
    log_sig = jnp.minimum(w, 0.0) - l1p
    log_1m = log_sig - w
    if valid is not None:
        log_1m = jnp.where(valid, log_1m, 0.0)
    return log_sig, log_1m


def _lanes(x, n):
    reps = n // LANES
    return x if reps == 1 or x.shape[1] == 1 else jnp.concatenate([x] * reps, axis=1)


def _sb_weights(log_sig, tail, carry, valid):
    a = jnp.exp2(log_sig + tail + _lanes(carry, log_sig.shape[1]))
    if valid is not None:
        a = jnp.where(valid, a, 0.0)
    return a


def _sb_prompt_kernel(q_ref, k_ref, v_ref, o_ref, carry_ref, acc_ref, *, blk, heads, scale2):
    i = pl.program_id(2)
    d = SB_HEAD_DIM
    u = _suffix_matrix(blk)
    row = lax.broadcasted_iota(jnp.int32, (blk, blk), 0)
    col = lax.broadcasted_iota(jnp.int32, (blk, blk), 1)
    carry_ref[...] = jnp.zeros_like(carry_ref)
    acc_ref[...] = jnp.zeros_like(acc_ref)

    def tile(j, valid):
        start = pl.multiple_of(j * blk, blk)
        hss = [slice(h * d, (h + 1) * d) for h in range(heads)]
        ws = [_dot_nt(q_ref[:, hs], k_ref[pl.ds(start, blk), hs]) for hs in hss]
        logs = [_sb_logs(w * scale2, valid) for w in ws]
        tails = [_suffix_sum(log_1m, u) for _, log_1m in logs]
        probs = []
        for h in range(heads):
            log_sig, log_1m = logs[h]
            probs.append(_sb_weights(log_sig, tails[h], carry_ref[h], valid).astype(BF16))
            carry_ref[h] += jnp.sum(log_1m, axis=1, keepdims=True)
        for h, hs in enumerate(hss):
            acc_ref[:, hs] += _dot(probs[h], v_ref[pl.ds(start, blk), hs])

    tile(i, col < row)

    def body(jj, c):
        tile(i - 1 - jj, None)
        return c

    lax.fori_loop(0, i, body, 0)
    o_ref[...] = acc_ref[...].astype(o_ref.dtype)


def _sb_attend_prompt(q, k, v, *, B, S, blk, heads):
    M, D = q.shape
    wblk = heads * SB_HEAD_DIM
    nq = S // blk
    qspec = pl.BlockSpec((blk, wblk), lambda b, h, i: (b * nq + i, h))
    kvspec = pl.BlockSpec((S, wblk), lambda b, h, i: (b, h))
    return pl.pallas_call(
        functools.partial(_sb_prompt_kernel, blk=blk, heads=heads, scale2=SB_HEAD_DIM ** -0.5 * LOG2E),
        grid=(B, D // wblk, nq),
        in_specs=[qspec, kvspec, kvspec],
        out_specs=qspec,
        out_shape=jax.ShapeDtypeStruct((M, D), BF16),
        scratch_shapes=[pltpu.VMEM((heads, blk, 1), F32),
                        pltpu.VMEM((blk, wblk), F32)],
        compiler_params=_cparams(("parallel", "parallel", "arbitrary"), _ATTN_FLAGS),
        name="sb_prompt",
    )(q, k, v)


def _diff_lambda(lam_ref, lam_init):
    lv = lam_ref[...]
    s1 = jnp.sum(lv[0:1] * lv[1:2], axis=1, keepdims=True)
    s2 = jnp.sum(lv[2:3] * lv[3:4], axis=1, keepdims=True)
    return jnp.exp(s1) - jnp.exp(s2) + lam_init


def _softmax_step(w, m, l):
    m_new = jnp.maximum(m, jnp.max(w, axis=1, keepdims=True))
    alpha = jnp.exp2(m - m_new)
    p = jnp.exp2(w - _lanes(m_new, w.shape[1]))
    return p, alpha, m_new, alpha * l + jnp.sum(p, axis=1, keepdims=True)


def _head_norm(o, g, lam_init):
    ms = jnp.mean(o * o, axis=1, keepdims=True)
    return o * lax.rsqrt(ms + SUBLN_EPS) * g * (1.0 - lam_init)


def _diff_prompt_kernel(q_ref, k_ref, v_ref, lam_ref, g_ref, o_ref, m_ref, l_ref, acc_ref, *,
                        blk, heads, scale2, lam_init):
    i = pl.program_id(2)
    d = DIFF_HEAD_DIM
    row = lax.broadcasted_iota(jnp.int32, (blk, blk), 0)
    col = lax.broadcasted_iota(jnp.int32, (blk, blk), 1)
    m_ref[...] = jnp.full(m_ref.shape, NEG_BIG, F32)
    l_ref[...] = jnp.zeros_like(l_ref)
    acc_ref[...] = jnp.zeros_like(acc_ref)

    def tile(j, valid):
        start = pl.multiple_of(j * blk, blk)
        chains = range(2 * heads)
        css = [slice(c * d, (c + 1) * d) for c in chains]
        probs, alphas = {}, {}
        for t in range(len(chains) + PV_SKEW):
            if t < len(chains):
                c = t
                w = _dot_nt(q_ref[:, css[c]], k_ref[pl.ds(start, blk), css[c]]) * scale2
                if valid is not None:
                    w = jnp.where(valid, w, NEG_BIG)
                p, al, m_new, l_new = _softmax_step(w, m_ref[c], l_ref[c])
                m_ref[c] = m_new
                l_ref[c] = l_new
                probs[c] = p.astype(BF16)
                alphas[c] = al
            c = t - PV_SKEW
            if 0 <= c < len(chains):
                h = c // 2
                pv = _dot(probs.pop(c), v_ref[pl.ds(start, blk), h * 2 * d:(h + 1) * 2 * d])
                acc_ref[c] = _lanes(alphas.pop(c), 2 * d) * acc_ref[c] + pv

    tile(i, col <= row)

    def body(jj, c):
        tile(i - 1 - jj, None)
        return c

    lax.fori_loop(0, i, body, 0)
    lam = _diff_lambda(lam_ref, lam_init)
    g = g_ref[...]
    for h in range(heads):
        o = (acc_ref[2 * h] / _lanes(l_ref[2 * h], 2 * d)
             - lam * (acc_ref[2 * h + 1] / _lanes(l_ref[2 * h + 1], 2 * d)))
        o_ref[:, h * 2 * d:(h + 1) * 2 * d] = _head_norm(o, g, lam_init).astype(o_ref.dtype)


def _diff_attend_prompt(q, k, v, lam_par, g, *, B, S, blk, heads, lam_init):
    M, D = q.shape
    wblk = heads * 2 * DIFF_HEAD_DIM
    nq = S // blk
    qspec = pl.BlockSpec((blk, wblk), lambda b, h, i: (b * nq + i, h))
    kvspec = pl.BlockSpec((S, wblk), lambda b, h, i: (b, h))
    return pl.pallas_call(
        functools.partial(_diff_prompt_kernel, blk=blk, heads=heads,
                          scale2=DIFF_HEAD_DIM ** -0.5 * LOG2E, lam_init=lam_init),
        grid=(B, D // wblk, nq),
        in_specs=[qspec, kvspec, kvspec,
                  pl.BlockSpec(lam_par.shape, lambda b, h, i: (0, 0)),
                  pl.BlockSpec(g.shape, lambda b, h, i: (0, 0))],
        out_specs=qspec,
        out_shape=jax.ShapeDtypeStruct((M, D), BF16),
        scratch_shapes=[pltpu.VMEM((2 * heads, blk, LANES), F32),
                        pltpu.VMEM((2 * heads, blk, LANES), F32),
                        pltpu.VMEM((2 * heads, blk, 2 * DIFF_HEAD_DIM), F32)],
        compiler_params=_cparams(("parallel", "parallel", "arbitrary"), _ATTN_FLAGS),
        name="diff_prompt",
    )(q, k, v, lam_par, g)


def _head_views(cache_ref, layer, pid):
    if len(cache_ref.shape) == 5:
        return [cache_ref.at[layer, pid, :, h, :] for h in range(cache_ref.shape[3])]
    return [cache_ref.at[layer, pid, :, h, m, :]
            for h in range(cache_ref.shape[3]) for m in range(cache_ref.shape[4])]


def _decode_kernel(*refs, diff, T, pages_per_step, page, layer, n_pages, scale2, lam_init):
    G = pages_per_step
    if diff:
        (pt_ref, q_ref, kn_ref, vn_ref, lam_ref, g_ref), rest = refs[:6], refs[6:]
    else:
        (pt_ref, q_ref, kn_ref, vn_ref), rest = refs[:4], refs[4:]
    ck_ref, cv_ref, o_ref, qbd_ref, acc_ref, s0_ref, s1_ref, kpad_ref, vpad_ref, kbuf, vbuf, sem = rest
    step = pl.program_id(1)
    n_steps = n_pages // G
    total = pl.num_programs(0) * n_steps
    lin = pl.program_id(0) * n_steps + step
    slot = lax.rem(lin, 2)
    R, D = qbd_ref.shape
    slot_w = D // (R // T)
    out_w = 2 * slot_w if diff else slot_w
    u = None if diff else _suffix_matrix(page)

    def page_copies(lin_step, sl):
        bb = lax.div(lin_step, n_steps)
        first = n_pages - (lax.rem(lin_step, n_steps) + 1) * G
        cps = []
        for c in range(G):
            pid = pt_ref[bb, first + c]
            for c_ref, buf, si in ((ck_ref, kbuf, 0), (cv_ref, vbuf, 1)):
                views = _head_views(c_ref, layer, pid)
                w = D // len(views)
                for n, src in enumerate(views):
                    dst = buf.at[sl, pl.ds(c * page, page), pl.ds(n * w, w)]
                    cps.append(pltpu.make_async_copy(src, dst, sem.at[si, sl]))
        return cps

    @pl.when(lin == 0)
    def _():
        for cp in page_copies(lin, slot):
            cp.start()

    for cp in page_copies(lax.rem(lin + 1, total), 1 - slot):
        cp.start()

    def attend(kf, vf, valid):
        n = kf.shape[0]
        w = _dot_nt(qbd_ref[...], kf.astype(BF16)) * scale2
        vb = vf.astype(BF16)
        if diff:
            if valid is not None:
                w = jnp.where(valid, w, NEG_BIG)
            p, alpha, m_new, l_new = _softmax_step(w, s0_ref[...], s1_ref[...])
            s0_ref[...] = m_new
            s1_ref[...] = l_new
            acc_ref[...] = _lanes(alpha, D) * acc_ref[...] + _dot(p.astype(BF16), vb)
        else:
            log_sig, log_1m = _sb_logs(w, valid)
            chunks = [slice(c * page, (c + 1) * page) for c in range(n // page)]
            carry = s0_ref[...]
            parts = [None] * len(chunks)
            for c in reversed(range(len(chunks))):
                sl = chunks[c]
                tail = _suffix_sum(log_1m[:, sl], u)
                parts[c] = _sb_weights(log_sig[:, sl], tail, carry, None if valid is None else valid[:, sl])
                carry = carry + jnp.sum(log_1m[:, sl], axis=1, keepdims=True)
            s0_ref[...] = carry
            a = parts[0] if len(parts) == 1 else jnp.concatenate(parts, axis=1)
            acc_ref[...] += _dot(a.astype(BF16), vb)

    @pl.when(step == 0)
    def _():
        rg = lax.broadcasted_iota(jnp.int32, (R, D), 0) // T
        cg = lax.broadcasted_iota(jnp.int32, (R, D), 1) // slot_w
        qbd_ref[...] = jnp.where(rg == cg, q_ref[...], jnp.zeros((), BF16))
        acc_ref[...] = jnp.zeros_like(acc_ref)
        s0_ref[...] = jnp.full(s0_ref.shape, NEG_BIG if diff else 0.0, F32)
        s1_ref[...] = jnp.zeros_like(s1_ref)
        kpad_ref[...] = jnp.zeros_like(kpad_ref)
        vpad_ref[...] = jnp.zeros_like(vpad_ref)
        kpad_ref[0:kn_ref.shape[0], :] = kn_ref[...]
        vpad_ref[0:vn_ref.shape[0], :] = vn_ref[...]
        t = lax.broadcasted_iota(jnp.int32, (R, page), 0) % T
        s = lax.broadcasted_iota(jnp.int32, (R, page), 1)
        valid = (s <= t) if diff else (s < t)
        attend(kpad_ref[...], vpad_ref[...], valid)

    for cp in page_copies(lin, slot):
        cp.wait()
    attend(kbuf[slot], vbuf[slot], None)

    @pl.when(lin == total - 1)
    def _():
        for cp in page_copies(lin * 0, 1 - slot):
            cp.wait()

    @pl.when(step == n_steps - 1)
    def _():
        acc = acc_ref[...]
        rg = lax.broadcasted_iota(jnp.int32, (R, D), 0)
        cg = lax.broadcasted_iota(jnp.int32, (R, D), 1) // out_w
        if diff:
            lam = _diff_lambda(lam_ref, lam_init)
            is_map1 = (lax.broadcasted_iota(jnp.int32, (R, LANES), 0) // T) % 2 == 0
            coef = jnp.where(is_map1, 1.0, -lam) / s1_ref[...]
            acc = jnp.where(rg // (2 * T) == cg, acc * _lanes(coef, D), 0.0)
        else:
            acc = jnp.where(rg // T == cg, acc, 0.0)
        n = R
        while n > SUBLANES:
            n //= 2
            acc = acc[:n] + acc[n:]
        assert SUBLANES % T == 0
        sh = T
        while sh < SUBLANES:
            acc = acc + pltpu.roll(acc, sh, 0)
            sh *= 2
        if diff:
            g = g_ref[...]
            for h in range(D // out_w):
                sl = slice(h * out_w, (h + 1) * out_w)
                o_ref[:, sl] = _head_norm(acc[:, sl], g, lam_init).astype(o_ref.dtype)
        else:
            o_ref[...] = acc.astype(o_ref.dtype)


def _decode_attend(q_rep, k_new, v_new, cache_k, cache_v, page_table, layer, *, T, diff,
                   lam_par=None, g=None, lam_init=0.0, pages_per_step):
    Bd, R, D = q_rep.shape
    page = cache_k.shape[2]
    n_pages = page_table.shape[1]
    G = pages_per_step
    n_steps = n_pages // G
    npad = k_new.shape[1]

    row_spec = lambda n: pl.BlockSpec((None, n, D), lambda b, s, pt: (b, 0, 0))
    in_specs = [row_spec(R), row_spec(npad), row_spec(npad)]
    args = [q_rep, k_new, v_new]
    if diff:
        in_specs += [pl.BlockSpec(lam_par.shape, lambda b, s, pt: (0, 0)),
                     pl.BlockSpec(g.shape, lambda b, s, pt: (0, 0))]
        args += [lam_par, g]
    in_specs += [pl.BlockSpec(memory_space=pl.ANY)] * 2
    args += [cache_k, cache_v]
    kern = functools.partial(_decode_kernel, diff=diff, T=T, pages_per_step=G, page=page, layer=layer,
                             n_pages=n_pages, scale2=(DIFF_HEAD_DIM if diff else SB_HEAD_DIM) ** -0.5 * LOG2E,
                             lam_init=lam_init)
    return pl.pallas_call(
        kern,
        grid_spec=pltpu.PrefetchScalarGridSpec(
            num_scalar_prefetch=1,
            grid=(Bd, n_steps),
            in_specs=in_specs,
            out_specs=row_spec(npad),
            scratch_shapes=[
                pltpu.VMEM((R, D), BF16),
                pltpu.VMEM((R, D), F32),
                pltpu.VMEM((R, LANES), F32),
                pltpu.VMEM((R, LANES), F32),
                pltpu.VMEM((page, D), F32),
                pltpu.VMEM((page, D), F32),
                pltpu.VMEM((2, G * page, D), F32),
                pltpu.VMEM((2, G * page, D), F32),
                pltpu.SemaphoreType.DMA((2, 2)),
            ]),
        out_shape=jax.ShapeDtypeStruct((Bd, npad, D), F32),
        compiler_params=_cparams(("arbitrary", "arbitrary")),
        name="diff_decode" if diff else "sb_decode",
    )(page_table, *args)


def _layer_norm(x, g, b):
    mu = jnp.mean(x, axis=1, keepdims=True)
    xc = x - mu
    var = jnp.mean(xc * xc, axis=1, keepdims=True)
    return xc * lax.rsqrt(var + LN_EPS) * g + b


def _oproj_ln_kernel(o_ref, w_ref, h_ref, g_ref, b_ref, hf_ref, hb_ref, *, alpha):
    x = alpha * h_ref[...] + _dot(o_ref[...], w_ref[...])
    y = _layer_norm(x, g_ref[...], b_ref[...])
    hf_ref[...] = y
    hb_ref[...] = y.astype(BF16)


def _oproj_ln(o, w, layer, h, g, b, *, tm, alpha):
    M, D = o.shape
    row = pl.BlockSpec((tm, D), lambda i: (i, 0))
    vec = pl.BlockSpec((None, 1, D), lambda i: (layer, 0, 0))
    return pl.pallas_call(
        functools.partial(_oproj_ln_kernel, alpha=alpha),
        grid=(M // tm,),
        in_specs=[row, pl.BlockSpec((None, D, D), lambda i: (layer, 0, 0)), row, vec, vec],
        out_specs=[row, row],
        out_shape=[jax.ShapeDtypeStruct((M, D), F32), jax.ShapeDtypeStruct((M, D), BF16)],
        compiler_params=_cparams(("parallel",)),
        name="oproj_ln",
    )(o, w, h, g, b)


def _ffn_ln_kernel(x_ref, wg_ref, wu_ref, wd_ref, h_ref, g_ref, b_ref, hf_ref, hb_ref, acc_ref, *, alpha):
    j = pl.program_id(1)

    @pl.when(j == 0)
    def _():
        acc_ref[...] = jnp.zeros_like(acc_ref)

    x = x_ref[...]
    gate = _dot(x, wg_ref[...])
    up = _dot(x, wu_ref[...])
    act = (gate * jax.nn.sigmoid(gate) * up).astype(BF16)
    acc_ref[...] += _dot(act, wd_ref[...])

    @pl.when(j == pl.num_programs(1) - 1)
    def _():
        y = _layer_norm(alpha * h_ref[...] + acc_ref[...], g_ref[...], b_ref[...])
        hf_ref[...] = y
        hb_ref[...] = y.astype(BF16)


def _ffn_ln(x, wgu, wd, layer, h, g, b, *, tm, tf, alpha):
    M, D = x.shape
    Fd = wd.shape[1]
    nf = Fd // tf
    row = pl.BlockSpec((tm, D), lambda i, j: (i, 0))
    vec = pl.BlockSpec((None, 1, D), lambda i, j: (layer, 0, 0))
    return pl.pallas_call(
        functools.partial(_ffn_ln_kernel, alpha=alpha),
        grid=(M // tm, nf),
        in_specs=[
            row,
            pl.BlockSpec((None, D, tf), lambda i, j: (layer, 0, j)),
            pl.BlockSpec((None, D, tf), lambda i, j: (layer, 0, j + nf)),
            pl.BlockSpec((None, tf, D), lambda i, j: (layer, j, 0)),
            row, vec, vec,
        ],
        out_specs=[row, row],
        out_shape=[jax.ShapeDtypeStruct((M, D), F32), jax.ShapeDtypeStruct((M, D), BF16)],
        scratch_shapes=[pltpu.VMEM((tm, D), F32)],
        compiler_params=_cparams(("parallel", "arbitrary")),
        name="ffn_ln",
    )(x, wgu, wgu, wd, h, g, b)


def _rope_tables(pos, d):
    half = d // 2
    inv = ROPE_THETA ** (-jnp.arange(half, dtype=F32) / half)
    ang = pos.astype(F32)[:, None] * inv[None, :]
    cos, sin = jnp.cos(ang), jnp.sin(ang)
    return jnp.concatenate([cos, cos], axis=1), jnp.concatenate([-sin, sin], axis=1)


def _pick(n, pref):
    for t in pref:
        if n % t == 0:
            return t
    return n


def kernel(x_prompt, x_sample, cache_sb_k, cache_sb_v, cache_diff_k, cache_diff_v, page_table,
           w_qkv, w_o, diff_lambda, diff_subln_g, ln_mix_g, ln_mix_b, w_gate_up, w_down,
           ln_ffn_g, ln_ffn_b):
    B, S, D = x_prompt.shape
    Bd, T, _ = x_sample.shape
    depth = w_qkv.shape[0]
    n_pages = page_table.shape[1]
    page = cache_sb_k.shape[2]
    P = n_pages * page
    Fd = w_down.shape[1]
    alpha = (2 * depth) ** 0.25
    Mp, Ms = B * S, Bd * T
    sb_heads = D // SB_HEAD_DIM
    diff_heads = D // (2 * DIFF_HEAD_DIM)

    tm_p = _pick(S, (1024, 512, 256, 128))
    tm_s = Ms
    tn = _pick(D, (512, 256, 128))
    blk_sb = _pick(S, (256, 128))
    blk_df = _pick(S, (512, 256, 128))
    tm_o = _pick(S, (256, 128))
    tm_f = _pick(S, (512, 256, 128))
    tf = _pick(Fd, (512, 256, 128))
    pages_per_step = _pick(n_pages, (8, 4, 2, 1))
    npad = SUBLANES
    R = 16 * T

    wqkv_b = w_qkv.astype(BF16)
    wo_b = w_o.astype(BF16)
    wgu_b = w_gate_up.astype(BF16)
    wd_b = w_down.astype(BF16)
    vecs = lambda a: a.reshape(depth, 1, D)
    mix_g, mix_b, ffn_g, ffn_b = vecs(ln_mix_g), vecs(ln_mix_b), vecs(ln_ffn_g), vecs(ln_ffn_b)

    tabs_p = _rope_tables(jnp.arange(S), DIFF_HEAD_DIM)
    tabs_s = _rope_tables(jnp.tile(P + jnp.arange(T), Bd), DIFF_HEAD_DIM)

    hf_p = x_prompt.reshape(Mp, D)
    hf_s = x_sample.reshape(Ms, D)
    hb_p = hf_p.astype(BF16)
    hb_s = hf_s.astype(BF16)

    def pad_rows(a):
        return jnp.pad(a.reshape(Bd, T, D), ((0, 0), (0, npad - T), (0, 0)))

    outs = {k: [] for k in ("sb_kp", "sb_vp", "sb_ks", "sb_vs", "df_kp", "df_vp", "df_ks", "df_vs")}
    for i in range(depth):
        j = i // N_MIXERS
        diff = i % N_MIXERS == 1
        q_p, kf_p, kb_p, vf_p, vb_p = _qkv_proj(hb_p, wqkv_b, i, tabs_p if diff else None,
                                                tm=tm_p, tn=tn, n_pos_blocks=S // tm_p)
        q_s, kf_s, _, vf_s, _ = _qkv_proj(hb_s, wqkv_b, i, tabs_s if diff else None,
                                          tm=tm_s, tn=tn, n_pos_blocks=1)
        q_rep = jnp.tile(q_s.reshape(Bd, 1, T, D), (1, R // T, 1, 1)).reshape(Bd, R, D)
        if diff:
            lam_init = 0.8 - 0.6 * math.exp(-0.3 * i)
            g_sub = diff_subln_g[j].reshape(1, 2 * DIFF_HEAD_DIM)
            o_p = _diff_attend_prompt(q_p, kb_p, vb_p, diff_lambda[j], g_sub, B=B, S=S, blk=blk_df,
                                      heads=2, lam_init=lam_init)
            o_s = _decode_attend(q_rep, pad_rows(kf_s), pad_rows(vf_s), cache_diff_k, cache_diff_v,
                                 page_table, j, T=T, diff=True, lam_par=diff_lambda[j], g=g_sub,
                                 lam_init=lam_init, pages_per_step=pages_per_step)
            outs["df_kp"].append(kf_p.reshape(B, S, diff_heads, 2, DIFF_HEAD_DIM))
            outs["df_vp"].append(vf_p.reshape(B, S, diff_heads, 2 * DIFF_HEAD_DIM))
            outs["df_ks"].append(kf_s.reshape(Bd, T, diff_heads, 2, DIFF_HEAD_DIM))
            outs["df_vs"].append(vf_s.reshape(Bd, T, diff_heads, 2 * DIFF_HEAD_DIM))
        else:
            o_p = _sb_attend_prompt(q_p, kb_p, vb_p, B=B, S=S, blk=blk_sb, heads=8)
            o_s = _decode_attend(q_rep, pad_rows(kf_s), pad_rows(vf_s), cache_sb_k, cache_sb_v,
                                 page_table, j, T=T, diff=False, pages_per_step=pages_per_step)
            outs["sb_kp"].append(kf_p.reshape(B, S, sb_heads, SB_HEAD_DIM))
            outs["sb_vp"].append(vf_p.reshape(B, S, sb_heads, SB_HEAD_DIM))
            outs["sb_ks"].append(kf_s.reshape(Bd, T, sb_heads, SB_HEAD_DIM))
            outs["sb_vs"].append(vf_s.reshape(Bd, T, sb_heads, SB_HEAD_DIM))
        o_s = o_s[:, :T].reshape(Ms, D).astype(BF16)
        hf_p, hb_p = _oproj_ln(o_p, wo_b, i, hf_p, mix_g, mix_b, tm=tm_o, alpha=alpha)
        hf_s, hb_s = _oproj_ln(o_s, wo_b, i, hf_s, mix_g, mix_b, tm=tm_s, alpha=alpha)
        hf_p, hb_p = _ffn_ln(hb_p, wgu_b, wd_b, i, hf_p, ffn_g, ffn_b, tm=tm_f, tf=tf, alpha=alpha)
        hf_s, hb_s = _ffn_ln(hb_s, wgu_b, wd_b, i, hf_s, ffn_g, ffn_b, tm=tm_s, tf=tf, alpha=alpha)

    st = lambda k: jnp.stack(outs[k])
    return (hf_p.reshape(B, S, D), hf_s.reshape(Bd, T, D),
            st("sb_kp"), st("sb_vp"), st("df_kp"), st("df_vp"),
            st("sb_ks"), st("sb_vs"), st("df_ks"), st("df_vs"))
```
